```python
import jax
import jax.numpy as jnp
from jax import lax
import numpy as np

D_MODEL = 1024
BATCH = 16
SEQ = 256
DEPTH = 1
DEC_BATCH = 4
DEC_SEQ = 1024
PAST_LEN = 512

GRID_W = 64
EPS = 1e-6
A_HEADS = 8
A_HEAD_DIM = 64
A_WIDTH = A_HEADS * A_HEAD_DIM
A_DECAY_RANK = 64
A_ICLR_RANK = 64
A_GATE_RANK = 128
A_GN_EPS = 64e-5
B_HEADS = 4
B_KEY_DIM = 64
B_VAL_DIM = 128
B_KEY_WIDTH = B_HEADS * B_KEY_DIM
B_VAL_WIDTH = B_HEADS * B_VAL_DIM
B_GATE_RANK = 16
GLA_GATE_NORM = 16.0
GLA_CHUNK = 64
A_COL_SIZES = (A_WIDTH, A_WIDTH, A_WIDTH, A_DECAY_RANK, A_DECAY_RANK, A_ICLR_RANK, A_ICLR_RANK, A_GATE_RANK)
B_COL_SIZES = (B_KEY_WIDTH, B_KEY_WIDTH, B_VAL_WIDTH, B_VAL_WIDTH, B_GATE_RANK, B_GATE_RANK)
A_COLS = sum(A_COL_SIZES)
B_COLS = sum(B_COL_SIZES)
IN_COLS = A_COLS + B_COLS + 2 * D_MODEL
MOE_GROUPS = 4
MOE_EXPERTS_PER_GROUP = 8
N_EXPERTS = MOE_GROUPS * MOE_EXPERTS_PER_GROUP
MOE_TOP_K = 2
EXPERT_HIDDEN = 512

kernel_name = 'hybrid_rwkv7_gla_hmoe_diffusion_step'


def split_cols(t, sizes):
    idx = np.cumsum(np.array(sizes))[:-1].tolist()
    return jnp.split(t, idx, axis=-1)


def rms_norm(x, g):
    x32 = x.astype(jnp.float32)
    y = x32 * lax.rsqrt(jnp.mean(x32 * x32, axis=-1, keepdims=True) + EPS)
    return (y * g).astype(x.dtype)


def token_shift(p, on_grid):
    bsz, t_len, ch = p.shape
    q0, q1, q2, q3 = jnp.split(p, 4, axis=-1)
    if on_grid:
        rows = t_len // GRID_W
        gr = lambda t: t.reshape(bsz, rows, GRID_W, t.shape[-1])
        left = jnp.pad(gr(q0)[:, :, :-1], ((0, 0), (0, 0), (1, 0), (0, 0)))
        right = jnp.pad(gr(q1)[:, :, 1:], ((0, 0), (0, 0), (0, 1), (0, 0)))
        up = jnp.pad(gr(q2)[:, :-1], ((0, 0), (1, 0), (0, 0), (0, 0)))
        down = jnp.pad(gr(q3)[:, 1:], ((0, 0), (0, 1), (0, 0), (0, 0)))
        return jnp.concatenate([left, right, up, down], axis=-1).reshape(bsz, t_len, ch)
    prev = lambda t: jnp.pad(t[:, :-1], ((0, 0), (1, 0), (0, 0)))
    nxt = lambda t: jnp.pad(t[:, 1:], ((0, 0), (0, 1), (0, 0)))
    return jnp.concatenate([prev(q0), nxt(q1), prev(q2), nxt(q3)], axis=-1)


def rwkv7_scan(r, w, k, v, a, b, s0):
    tm = lambda t: jnp.moveaxis(t.astype(jnp.float32), 1, 0)

    def step(S, inp):
        r_t, w_t, k_t, v_t, a_t, b_t = inp
        sa = jnp.einsum('bhvk,bhk->bhv', S, a_t)
        S = S * w_t[:, :, None, :] + v_t[..., :, None] * k_t[..., None, :] + sa[..., :, None] * b_t[..., None, :]
        return S, jnp.einsum('bhvk,bhk->bhv', S, r_t)

    s_fin, o = lax.scan(step, s0.astype(jnp.float32), (tm(r), tm(w), tm(k), tm(v), tm(a), tm(b)))
    return jnp.moveaxis(o, 0, 1), s_fin


def rwkv7_mixer(pa, s0, lp):
    bsz, t_len, _ = pa.shape
    r, k, v, wd_f, wd_b, ad_f, ad_b, gd = split_cols(pa, A_COL_SIZES)
    heads = lambda t: t.reshape(bsz, t_len, A_HEADS, A_HEAD_DIM)
    g = jax.nn.sigmoid(gd) @ lp['a_g_up']
    kk = heads(k * lp['a_k_k']).astype(jnp.float32)
    kk = kk / jnp.maximum(jnp.sqrt(jnp.sum(kk * kk, axis=-1, keepdims=True)), 1e-12)
    rh, vh = heads(r), heads(v)
    outs, bonus, finals = [], [], []
    for d, (wd, ad) in enumerate(((wd_f, ad_f), (wd_b, ad_b))):
        w_log = -jax.nn.softplus(-(lp['a_w0'][d] + jnp.tanh(wd) @ lp['a_w_up'][d])) - 0.5
        decay = heads(jnp.exp(-jnp.exp(w_log)))
        a = jax.nn.sigmoid(lp['a_a0'][d] + ad @ lp['a_a_up'][d])
        kdh = heads(k * (1 + (a - 1) * lp['a_k_a']))
        seq = (rh, decay, kdh, vh, -kk, kk * heads(a))
        if d == 1:
            seq = tuple(jnp.flip(t, axis=1) for t in seq)
        o, s_fin = rwkv7_scan(*seq, s0[:, d])
        if d == 1:
            o = jnp.flip(o, axis=1)
        outs.append(o)
        finals.append(s_fin)
        bonus.append(jnp.sum(rh * kdh * lp['a_r_k'], axis=-1, keepdims=True) * vh)
    o = outs[0] + outs[1]
    mu = jnp.mean(o, axis=-1, keepdims=True)
    var = jnp.mean(jnp.square(o - mu), axis=-1, keepdims=True)
    o = ((o - mu) * lax.rsqrt(var + A_GN_EPS)).reshape(bsz, t_len, A_WIDTH) * lp['a_ln_g'] + lp['a_ln_b']
    o = o + (bonus[0] + bonus[1]).reshape(bsz, t_len, A_WIDTH)
    return (o * g).astype(pa.dtype), jnp.stack(finals, axis=1).astype(pa.dtype)


def gla_chunked(q, k, v, gk, s0):
    bsz, t_len, nh, dk = q.shape
    dv = v.shape[-1]
    n_chunks = t_len // GLA_CHUNK
    f = lambda t: t.astype(jnp.float32).reshape(bsz, n_chunks, GLA_CHUNK, nh, t.shape[-1])
    q, k, v, gk = f(q), f(k), f(v), f(gk)
    b = jnp.cumsum(gk, axis=2)
    qe = q * jnp.exp(b)
    ke = k * jnp.exp(-b)
    causal = jnp.tril(jnp.ones((GLA_CHUNK, GLA_CHUNK), dtype=bool))
    att = jnp.where(causal, jnp.einsum('bnihd,bnjhd->bnhij', qe, ke), 0.0)
    o_intra = jnp.einsum('bnhij,bnjhe->bnihe', att, v)
    b_last = b[:, :, -1]
    chunk_kv = jnp.einsum('bnchd,bnche->bnhde', k * jnp.exp(b_last[:, :, None] - b), v)

    def step(S, inp):
        qe_c, bl, kv = inp
        o = jnp.einsum('bchd,bhde->bche', qe_c, S)
        return jnp.exp(bl)[..., None] * S + kv, o

    s_fin, o_inter = lax.scan(step, s0.astype(jnp.float32),
                              (jnp.moveaxis(qe, 1, 0), jnp.moveaxis(b_last, 1, 0), jnp.moveaxis(chunk_kv, 1, 0)))
    o = o_intra + jnp.moveaxis(o_inter, 0, 1)
    return o.reshape(bsz, t_len, nh, dv), s_fin


def gla_mixer(pb, s0, lp):
    bsz, t_len, _ = pb.shape
    q, k, v, g, gd_f, gd_b = split_cols(pb, B_COL_SIZES)
    kh = lambda t: t.reshape(bsz, t_len, B_HEADS, B_KEY_DIM)
    vh = lambda t: t.reshape(bsz, t_len, B_HEADS, B_VAL_DIM)
    qh, khh, vhh = kh(q) * (B_KEY_DIM ** -0.5), kh(k), vh(v)
    outs, finals = [], []
    for d, gd in enumerate((gd_f, gd_b)):
        gk = kh(jax.nn.log_sigmoid(gd @ lp['b_gk_up'][d] + lp['b_gk_bias'][d]) / GLA_GATE_NORM)
        seq = (qh, khh, vhh, gk)
        if d == 1:
            seq = tuple(jnp.flip(t, axis=1) for t in seq)
        o, s_fin = gla_chunked(*seq, s0[:, d])
        if d == 1:
            o = jnp.flip(o, axis=1)
        outs.append(o)
        finals.append(s_fin)
    o = outs[0] + outs[1]
    o = o * lax.rsqrt(jnp.mean(o * o, axis=-1, keepdims=True) + EPS) * lp['b_norm_g']
    y = (o * jax.nn.silu(vh(g))).reshape(bsz, t_len, B_VAL_WIDTH)
    return y.astype(pb.dtype), jnp.stack(finals, axis=1).astype(pb.dtype)


def hier_moe(h, lp):
    bsz, t_len, dm = h.shape
    hf = h.reshape(bsz * t_len, dm)
    g_logits = (hf @ lp['w_router_group'] + lp['b_router_group']).astype(jnp.float32)
    g_prob = jax.nn.softmax(g_logits, axis=-1)
    grp = jnp.argmax(g_logits, axis=-1)
    g_w = jnp.take_along_axis(g_prob, grp[:, None], axis=-1)
    e_logits = jnp.einsum('nd,gde->nge', hf, lp['w_router_expert']) + lp['b_router_expert']
    e_logits = jnp.take_along_axis(e_logits, grp[:, None, None], axis=1)[:, 0].astype(jnp.float32)
    top_v, top_i = lax.top_k(e_logits, MOE_TOP_K)
    wts = jax.nn.softmax(top_v, axis=-1) * g_w
    eid = grp[:, None] * MOE_EXPERTS_PER_GROUP + top_i
    gates = jnp.sum(jax.nn.one_hot(eid, N_EXPERTS, dtype=jnp.float32) * wts[..., None], axis=1).astype(h.dtype)
    hg = jnp.einsum('nd,edf->nef', hf, lp['w_exp_gate'])
    hu = jnp.einsum('nd,edf->nef', hf, lp['w_exp_up'])
    act = jax.nn.silu(hg) * hu * gates[..., None]
    out = jnp.einsum('nef,efd->nd', act, lp['w_exp_down'])
    return out.reshape(bsz, t_len, dm)


def trunk_layer(x, mod, on_grid, s_rwkv, s_gla, lp):
    sh1, sc1, g1, sh2, sc2, g2 = jnp.split(mod, 6, axis=-1)
    h = rms_norm(x, lp['norm1_g']) * (1 + sc1) + sh1
    p = h @ lp['w_in']
    pa, pb, pg = split_cols(p, (A_COLS, B_COLS, 2 * D_MODEL))
    pa = pa + (token_shift(pa, on_grid) - pa) * lp['shift_mu']
    ya, sa = rwkv7_mixer(pa, s_rwkv, lp)
    yb, sb = gla_mixer(pb, s_gla, lp)
    ga, gb = jnp.split(jax.nn.sigmoid(pg), 2, axis=-1)
    mix = (ga * (ya @ lp['w_out_a']) + gb * (yb @ lp['w_out_b'])) @ lp['w_o']
    x = x + g1 * mix
    h2 = rms_norm(x, lp['norm2_g']) * (1 + sc2) + sh2
    x = x + g2 * hier_moe(h2, lp)
    return x, sa, sb


def setup_inputs(seed: int = 0) -> dict:
    key = jax.random.key(seed)
    ks = iter(jax.random.split(key, 40))
    nrm = lambda shape, s: jax.random.normal(next(ks), shape, jnp.float32) * s
    uni = lambda shape, lo, hi: jax.random.uniform(next(ks), shape, jnp.float32, minval=lo, maxval=hi)
    L, D = DEPTH, D_MODEL
    return {
        'x_prompt': nrm((BATCH, SEQ, D), 1.0),
        'x_sample': nrm((DEC_BATCH, DEC_SEQ, D), 1.0),
        'state_rwkv': nrm((DEC_BATCH, L, 2, A_HEADS, A_HEAD_DIM, A_HEAD_DIM), 0.5),
        'state_gla': nrm((DEC_BATCH, L, 2, B_HEADS, B_KEY_DIM, B_VAL_DIM), 0.5),
        'c': nrm((DEC_BATCH, D), 1.0),
        'c_ctx': nrm((D,), 1.0),
        'w_mod': nrm((L, D, 6 * D), 0.5 * D ** -0.5),
        'b_mod': nrm((L, 6 * D), 0.01),
        'norm1_g': 1.0 + nrm((L, D), 0.05),
        'norm2_g': 1.0 + nrm((L, D), 0.05),
        'w_in': nrm((L, D, IN_COLS), D ** -0.5),
        'shift_mu': uni((L, A_COLS), 0.0, 1.0),
        'a_w0': uni((L, 2, A_WIDTH), -6.0, 1.0),
        'a_w_up': nrm((L, 2, A_DECAY_RANK, A_WIDTH), 0.5 * A_DECAY_RANK ** -0.5),
        'a_a0': nrm((L, 2, A_WIDTH), 0.1),
        'a_a_up': nrm((L, 2, A_ICLR_RANK, A_WIDTH), 0.5 * A_ICLR_RANK ** -0.5),
        'a_g_up': nrm((L, A_GATE_RANK, A_WIDTH), A_GATE_RANK ** -0.5),
        'a_k_k': 0.85 + nrm((L, A_WIDTH), 0.05),
        'a_k_a': 1.0 + nrm((L, A_WIDTH), 0.05),
        'a_r_k': nrm((L, A_HEADS, A_HEAD_DIM), 0.1),
        'a_ln_g': 1.0 + nrm((L, A_WIDTH), 0.05),
        'a_ln_b': nrm((L, A_WIDTH), 0.01),
        'b_gk_up': nrm((L, 2, B_GATE_RANK, B_KEY_WIDTH), B_GATE_RANK ** -0.5),
        'b_gk_bias': nrm((L, 2, B_KEY_WIDTH), 0.1),
        'b_norm_g': 1.0 + nrm((L, B_VAL_DIM), 0.05),
        'w_out_a': nrm((L, A_WIDTH, D), A_WIDTH ** -0.5),
        'w_out_b': nrm((L, B_VAL_WIDTH, D), B_VAL_WIDTH ** -0.5),
        'w_o': nrm((L, D, D), D ** -0.5),
        'w_router_group': nrm((L, D, MOE_GROUPS), D ** -0.5),
        'b_router_group': nrm((L, MOE_GROUPS), 0.01),
        'w_router_expert': nrm((L, MOE_GROUPS, D, MOE_EXPERTS_PER_GROUP), D ** -0.5),
        'b_router_expert': nrm((L, MOE_GROUPS, MOE_EXPERTS_PER_GROUP), 0.01),
        'w_exp_gate': nrm((L, N_EXPERTS, D, EXPERT_HIDDEN), D ** -0.5),
        'w_exp_up': nrm((L, N_EXPERTS, D, EXPERT_HIDDEN), D ** -0.5),
        'w_exp_down': nrm((L, N_EXPERTS, EXPERT_HIDDEN, D), EXPERT_HIDDEN ** -0.5),
        'final_norm_g': 1.0 + nrm((D,), 0.05),
    }


def reference(x_prompt, x_sample, state_rwkv, state_gla, c, c_ctx, w_mod, b_mod, norm1_g, norm2_g,
              w_in, shift_mu, a_w0, a_w_up, a_a0, a_a_up, a_g_up, a_k_k, a_k_a, a_r_k, a_ln_g, a_ln_b,
              b_gk_up, b_gk_bias, b_norm_g, w_out_a, w_out_b, w_o, w_router_group, b_router_group,
              w_router_expert, b_router_expert, w_exp_gate, w_exp_up, w_exp_down, final_norm_g):
    xp, xs = x_prompt, x_sample
    n_ctx = x_prompt.shape[0]
    rwkv_states, gla_states = [], []
    for l in range(DEPTH):
        lp = {
            'norm1_g': norm1_g[l], 'norm2_g': norm2_g[l], 'w_in': w_in[l], 'shift_mu': shift_mu[l],
            'a_w0': a_w0[l], 'a_w_up': a_w_up[l], 'a_a0': a_a0[l], 'a_a_up': a_a_up[l],
            'a_g_up': a_g_up[l], 'a_k_k': a_k_k[l], 'a_k_a': a_k_a[l], 'a_r_k': a_r_k[l],
            'a_ln_g': a_ln_g[l], 'a_ln_b': a_ln_b[l],
            'b_gk_up': b_gk_up[l], 'b_gk_bias': b_gk_bias[l], 'b_norm_g': b_norm_g[l],
            'w_out_a': w_out_a[l], 'w_out_b': w_out_b[l], 'w_o': w_o[l],
            'w_router_group': w_router_group[l], 'b_router_group': b_router_group[l],
            'w_router_expert': w_router_expert[l], 'b_router_expert': b_router_expert[l],
            'w_exp_gate': w_exp_gate[l], 'w_exp_up': w_exp_up[l], 'w_exp_down': w_exp_down[l],
        }
        mod_ctx = (jax.nn.silu(c_ctx) @ w_mod[l] + b_mod[l])[None, None, :]
        mod_lat = (jax.nn.silu(c) @ w_mod[l] + b_mod[l])[:, None, :]
        zero_rwkv = jnp.zeros((n_ctx, 2, A_HEADS, A_HEAD_DIM, A_HEAD_DIM), x_prompt.dtype)
        zero_gla = jnp.zeros((n_ctx, 2, B_HEADS, B_KEY_DIM, B_VAL_DIM), x_prompt.dtype)
        xp, s_a, s_b = trunk_layer(xp, mod_ctx, False, zero_rwkv, zero_gla, lp)
        rwkv_states.append(s_a)
        gla_states.append(s_b)
        xs, _, _ = trunk_layer(xs, mod_lat, True, state_rwkv[:, l], state_gla[:, l], lp)
    y_prompt = rms_norm(xp, final_norm_g)
    y_sample = rms_norm(xs, final_norm_g)
    new_state_rwkv = jnp.stack(rwkv_states, axis=1)
    new_state_gla = jnp.stack(gla_states, axis=1)
    return (y_prompt, y_sample, new_state_rwkv, new_state_gla)
```

```python
import functools

import jax
import jax.numpy as jnp
from jax import lax
from jax.experimental import pallas as pl
from jax.experimental.pallas import tpu as pltpu

F32 = jnp.float32
BF16 = jnp.bfloat16
HI = lax.Precision.HIGHEST

LANES = 128
SUBLANES = 8
VMEM_LIMIT = 56 * 1024 * 1024

EPS = 1e-6
GN_EPS = 64e-5
GRID_W = 64
A_HEADS, A_DIM = 8, 64
A_WIDTH = A_HEADS * A_DIM
B_HEADS, B_KEY, B_VAL = 4, 64, 128
GLA_CHUNK = 64
GLA_GATE_NORM = 16.0
N_GROUPS, N_PER_GROUP = 4, 8
N_EXPERTS = N_GROUPS * N_PER_GROUP


def _cparams(sem):
    return pltpu.CompilerParams(dimension_semantics=sem, vmem_limit_bytes=VMEM_LIMIT)


def _log_sigmoid(z):
    return jnp.minimum(z, 0.0) - jnp.log(1.0 + jnp.exp(-jnp.abs(z)))


def _dot(a, b, precision=None):
    return jnp.dot(a, b, preferred_element_type=F32, precision=precision)


def _mod_kernel(c_ref, w_ref, b_ref, o_ref):
    c = c_ref[...]
    o_ref[...] = _dot(c * jax.nn.sigmoid(c), w_ref[...], HI) + b_ref[...]


def _modulation(cvec, w_mod, b_mod):
    rows, d = cvec.shape
    n = w_mod.shape[1]
    tn = n // 4
    return pl.pallas_call(
        _mod_kernel,
        grid=(n // tn,),
        in_specs=[pl.BlockSpec((rows, d), lambda j: (0, 0)),
                  pl.BlockSpec((d, tn), lambda j: (0, j)),
                  pl.BlockSpec((1, tn), lambda j: (0, j))],
        out_specs=pl.BlockSpec((rows, tn), lambda j: (0, j)),
        out_shape=jax.ShapeDtypeStruct((rows, n), F32),
        compiler_params=_cparams(("arbitrary",)),
    )(cvec, w_mod, b_mod)


def _mod_row(i, tm, n_ctx, t_lat):
    start = i * tm
    return jnp.where(start < n_ctx, 0, 1 + (start - n_ctx) // t_lat)


def _in_proj_kernel(x_ref, sh_ref, sc_ref, g_ref, wa_ref, wb_ref, wg_ref, pa_ref, pb_ref, pg_ref):
    x = x_ref[...]
    y = x * lax.rsqrt(jnp.mean(x * x, axis=-1, keepdims=True) + EPS)
    h = ((y * g_ref[...]) * (1.0 + sc_ref[0]) + sh_ref[0]).astype(BF16)
    pa_ref[...] = _dot(h, wa_ref[...])
    pb_ref[...] = _dot(h, wb_ref[...])
    pg_ref[...] = _dot(h, wg_ref[...])


def _in_proj(x, mod3, norm_g, wa, wb, wg, n_ctx, t_lat):
    n, d = x.shape
    tm = 256
    row = functools.partial(_mod_row, tm=tm, n_ctx=n_ctx, t_lat=t_lat)
    full = lambda w: pl.BlockSpec(w.shape, lambda i: (0, 0))
    outs = [jax.ShapeDtypeStruct((n, w.shape[1]), F32) for w in (wa, wb, wg)]
    return pl.pallas_call(
        _in_proj_kernel,
        grid=(n // tm,),
        in_specs=[pl.BlockSpec((tm, d), lambda i: (i, 0)),
                  pl.BlockSpec((1, 1, d), lambda i: (row(i), 0, 0)),
                  pl.BlockSpec((1, 1, d), lambda i: (row(i), 0, 1)),
                  pl.BlockSpec((1, d), lambda i: (0, 0)),
                  full(wa), full(wb), full(wg)],
        out_specs=[pl.BlockSpec((tm, w.shape[1]), lambda i: (i, 0)) for w in (wa, wb, wg)],
        out_shape=outs,
        compiler_params=_cparams(("parallel",)),
    )(x, mod3, mod3, norm_g, wa, wb, wg)


def _pair_heads(x, y):
    lane = lax.broadcasted_iota(jnp.int32, (1, LANES), 1)
    low = lane < A_DIM
    tiles = []
    for p in range(A_HEADS // 2):
        xv = x[:, p * LANES:(p + 1) * LANES]
        yv = y[:, p * LANES:(p + 1) * LANES]
        tiles.append(jnp.where(low, xv, pltpu.roll(yv, A_DIM, axis=1)))
        tiles.append(jnp.where(low, pltpu.roll(xv, A_DIM, axis=1), yv))
    return tiles


def _prep_a_kernel(pa_ref, prev_ref, next_ref, mu_ref, wup_ref, aup_ref, gup_ref, w0_ref, a0_ref, kk_ref, ka_ref,
                   rk_ref, seg_ref, ra_ref, wkf_ref, bvf_ref, wkb_ref, bvb_ref, bonus_ref, g_ref, *,
                   on_grid, seq_len):
    pa = pa_ref[...]
    t_len, cols = pa.shape
    halo = prev_ref.shape[0]
    ext = jnp.concatenate([prev_ref[...], pa, next_ref[...]], axis=0)
    col = lax.broadcasted_iota(jnp.int32, (1, cols), 1)
    row = lax.broadcasted_iota(jnp.int32, (t_len, 1), 0) + pl.program_id(1) * t_len
    quarter = col // (cols // 4)

    def shifted(s):
        if s % SUBLANES == 0:
            moved = ext[halo - s:halo - s + t_len]
        else:
            moved = pltpu.roll(ext, s % ext.shape[0], axis=0)[halo:halo + t_len]
        valid = (row >= s) & (row < seq_len + s)
        return jnp.where(valid, moved, 0.0)

    if on_grid:
        gcol = row % GRID_W
        left = jnp.where(gcol != 0, shifted(1), 0.0)
        right = jnp.where(gcol != GRID_W - 1, shifted(-1), 0.0)
        nb = jnp.where(quarter == 0, left,
                       jnp.where(quarter == 1, right,
                                 jnp.where(quarter == 2, shifted(GRID_W), shifted(-GRID_W))))
    else:
        nb = jnp.where(quarter % 2 == 0, shifted(1), shifted(-1))
    pa = pa + (nb - pa) * mu_ref[...]

    r = pa[:, 0:A_WIDTH]
    k = pa[:, A_WIDTH:2 * A_WIDTH]
    v = pa[:, 2 * A_WIDTH:3 * A_WIDTH]
    c0 = 3 * A_WIDTH
    wd = pa[:, c0:c0 + LANES]
    ad = pa[:, c0 + LANES:c0 + 2 * LANES]
    gd = pa[:, c0 + 2 * LANES:c0 + 3 * LANES]
    seg = seg_ref[...]

    up_w = _dot(jnp.tanh(wd), wup_ref[...], HI)
    up_a = _dot(ad, aup_ref[...], HI)
    g_ref[...] = _dot(jax.nn.sigmoid(gd), gup_ref[...], HI)

    kk = k * kk_ref[...]
    kk = kk / jnp.maximum(jnp.sqrt(_dot(kk * kk, seg, HI)), 1e-12)
    neg_kk = -kk
    kd_sum = None
    wk_refs = (wkf_ref, wkb_ref)
    bv_refs = (bvf_ref, bvb_ref)
    for d in range(2):
        sl = slice(d * A_WIDTH, (d + 1) * A_WIDTH)
        w_log = _log_sigmoid(w0_ref[d:d + 1, :] + up_w[:, sl]) - 0.5
        decay = jnp.exp(-jnp.exp(w_log))
        a = jax.nn.sigmoid(a0_ref[d:d + 1, :] + up_a[:, sl])
        kd = k * (1.0 + (a - 1.0) * ka_ref[...])
        kd_sum = kd if kd_sum is None else kd_sum + kd
        for h, tile in enumerate(_pair_heads(decay, kd)):
            wk_refs[d][pl.ds(h, t_len, stride=A_HEADS), :] = tile
        for h, tile in enumerate(_pair_heads(kk * a, v)):
            bv_refs[d][pl.ds(h, t_len, stride=A_HEADS), :] = tile
    for h, tile in enumerate(_pair_heads(r, neg_kk)):
        ra_ref[pl.ds(h, t_len, stride=A_HEADS), :] = tile
    bonus_ref[...] = _dot(r * kd_sum * rk_ref[...], seg, HI) * v


def _prep_a(pa, row0, n_seq, t_len, on_grid, consts):
    n_rows, cols = pa.shape
    tt = 256
    halo = GRID_W if on_grid else SUBLANES
    tiles = t_len // tt
    blk0 = row0 // tt
    main = lambda i, j: (blk0 + i * tiles + j, 0)
    prev = lambda i, j: (jnp.maximum((blk0 + i * tiles + j) * (tt // halo) - 1, 0), 0)
    nxt = lambda i, j: (jnp.minimum((blk0 + i * tiles + j + 1) * (tt // halo), n_rows // halo - 1), 0)
    full = lambda w: pl.BlockSpec(w.shape, lambda i, j: (0,) * w.ndim)
    hr = jax.ShapeDtypeStruct((n_seq * t_len * A_HEADS, LANES), F32)
    tok = jax.ShapeDtypeStruct((n_seq * t_len, A_WIDTH), F32)
    hr_spec = pl.BlockSpec((tt * A_HEADS, LANES), lambda i, j: (i * tiles + j, 0))
    tok_spec = pl.BlockSpec((tt, A_WIDTH), lambda i, j: (i * tiles + j, 0))
    return pl.pallas_call(
        functools.partial(_prep_a_kernel, on_grid=on_grid, seq_len=t_len),
        grid=(n_seq, tiles),
        in_specs=[pl.BlockSpec((tt, cols), main), pl.BlockSpec((halo, cols), prev),
                  pl.BlockSpec((halo, cols), nxt)] + [full(w) for w in consts],
        out_specs=[hr_spec] * 5 + [tok_spec] * 2,
        out_shape=[hr] * 5 + [tok] * 2,
        compiler_params=_cparams(("parallel", "parallel")),
    )(pa, pa, pa, *consts)


def _scan_block(load_tile, store_o, s_ref, kra, kwk, kbv, tb, v_rows, lane_rep):
    def transpose_step(j, carry):
        kra[j] = load_tile(0, j).T
        kwk[j] = load_tile(1, j).T
        kbv[j] = load_tile(2, j).T
        return carry

    lax.fori_loop(0, tb, transpose_step, 0)
    kra[tb] = kra[tb - 1]

    def row_b(buf, j, r):
        return jnp.broadcast_to(buf[j, pl.ds(r, 1), :], (v_rows, LANES))

    n_kb = A_DIM // SUBLANES

    def sa_init(kb, acc):
        for kk in range(SUBLANES):
            k = kb * SUBLANES + kk
            acc = acc + s_ref[k] * row_b(kra, 0, A_DIM + k)
        return acc

    sa0 = lax.fori_loop(0, n_kb, sa_init, jnp.zeros((v_rows, LANES), F32))
    lane = lax.broadcasted_iota(jnp.int32, (1, LANES), 1)

    def step(j, sa):
        vt = kbv[j, pl.ds(A_DIM, A_DIM), :]
        if lane_rep == 1:
            v = vt
        else:
            v = jnp.where(lane < LANES // 2, vt[:v_rows], vt[v_rows:])

        def kblock(kb, carry):
            o, sa_next = carry
            for kk in range(SUBLANES):
                k = kb * SUBLANES + kk
                s_new = (s_ref[k] * row_b(kwk, j, k) + row_b(kwk, j, A_DIM + k) * v
                         + row_b(kbv, j, k) * sa)
                s_ref[k] = s_new
                o = o + s_new * row_b(kra, j, k)
                sa_next = sa_next + s_new * row_b(kra, j + 1, A_DIM + k)
            return o, sa_next

        zero = jnp.zeros((v_rows, LANES), F32)
        o, sa_next = lax.fori_loop(0, n_kb, kblock, (zero, zero))
        store_o(j, o)
        return sa_next

    lax.fori_loop(0, tb, step, sa0)


def _scan_ctx_kernel(ra_ref, wk_ref, bv_ref, o_ref, sfin_ref, s_ref, kra, kwk, kbv, *, tb, n_seq):
    g = pl.program_id(0)
    i = pl.program_id(1)

    @pl.when(i == 0)
    def _():
        s_ref[...] = jnp.zeros_like(s_ref)

    def local_t(j):
        return j + g * (tb - 1 - 2 * j)

    refs = (ra_ref, wk_ref, bv_ref)

    def load_tile(which, j):
        ref = refs[which]
        rows = pl.ds(pl.multiple_of(local_t(j) * A_HEADS, A_HEADS), A_HEADS)
        slabs = [ref[b, rows, :] if which == 0 else ref[0, b, rows, :] for b in range(n_seq)]
        return jnp.concatenate(slabs, axis=0)

    def store_o(j, o):
        o_ref[0, local_t(j)] = o

    _scan_block(load_tile, store_o, s_ref, kra, kwk, kbv, tb, A_DIM, 1)

    @pl.when(i == pl.num_programs(1) - 1)
    def _():
        sfin_ref[0] = s_ref[...]


def _scan_ctx(ra, wk, bv, n_seq, t_len, tb):
    nb = t_len // tb
    blk = lambda g, i: jnp.where(g == 0, i, nb - 1 - i)
    rows = tb * A_HEADS
    return pl.pallas_call(
        functools.partial(_scan_ctx_kernel, tb=tb, n_seq=n_seq),
        grid=(2, nb),
        in_specs=[pl.BlockSpec((n_seq, rows, LANES), lambda g, i: (0, blk(g, i), 0)),
                  pl.BlockSpec((1, n_seq, rows, LANES), lambda g, i: (g, 0, blk(g, i), 0)),
                  pl.BlockSpec((1, n_seq, rows, LANES), lambda g, i: (g, 0, blk(g, i), 0))],
        out_specs=[pl.BlockSpec((1, tb, A_DIM, LANES), lambda g, i: (g, blk(g, i), 0, 0)),
                   pl.BlockSpec((1, A_DIM, A_DIM, LANES), lambda g, i: (g, 0, 0, 0))],
        out_shape=[jax.ShapeDtypeStruct((2, t_len, A_DIM, LANES), F32),
                   jax.ShapeDtypeStruct((2, A_DIM, A_DIM, LANES), F32)],
        scratch_shapes=[pltpu.VMEM((A_DIM, A_DIM, LANES), F32),
                        pltpu.VMEM((tb + 1, LANES, LANES), F32),
                        pltpu.VMEM((tb, LANES, LANES), F32),
                        pltpu.VMEM((tb, LANES, LANES), F32)],
        compiler_params=_cparams(("arbitrary", "arbitrary")),
    )(ra, wk, bv)


def _scan_lat_kernel(raf_ref, rab_ref, wkf_ref, wkb_ref, bvf_ref, bvb_ref, s0_ref, of_ref, ob_ref,
                     s_ref, kra, kwk, kbv, *, tb, n_seq):
    i = pl.program_id(0)

    @pl.when(i == 0)
    def _():
        s_ref[...] = s0_ref[...]

    refs = ((raf_ref, rab_ref), (wkf_ref, wkb_ref), (bvf_ref, bvb_ref))

    def load_tile(which, j):
        fwd, bwd = refs[which]
        rows_f = pl.ds(pl.multiple_of(j * A_HEADS, A_HEADS), A_HEADS)
        rows_b = pl.ds(pl.multiple_of((tb - 1 - j) * A_HEADS, A_HEADS), A_HEADS)
        if which == 0:
            half = ([fwd[b, rows_f, :] for b in range(n_seq)] + [bwd[b, rows_b, :] for b in range(n_seq)])
        else:
            half = ([fwd[0, b, rows_f, :] for b in range(n_seq)] + [bwd[0, b, rows_b, :] for b in range(n_seq)])
        return jnp.concatenate(half + half, axis=0)

    def store_o(j, o):
        of_ref[j] = o
        ob_ref[tb - 1 - j] = o

    _scan_block(load_tile, store_o, s_ref, kra, kwk, kbv, tb, A_DIM // 2, 2)


def _scan_lat(ra, wk, bv, s0, n_seq, t_len, tb):
    nb = t_len // tb
    rows = tb * A_HEADS
    vl = A_DIM // 2
    f3 = pl.BlockSpec((n_seq, rows, LANES), lambda i: (0, i, 0))
    b3 = pl.BlockSpec((n_seq, rows, LANES), lambda i: (0, nb - 1 - i, 0))
    f4 = lambda d: pl.BlockSpec((1, n_seq, rows, LANES), lambda i: (d, 0, i, 0))
    b4 = lambda d: pl.BlockSpec((1, n_seq, rows, LANES), lambda i: (d, 0, nb - 1 - i, 0))
    o_shape = jax.ShapeDtypeStruct((t_len, vl, LANES), F32)
    return pl.pallas_call(
        functools.partial(_scan_lat_kernel, tb=tb, n_seq=n_seq),
        grid=(nb,),
        in_specs=[f3, b3, f4(0), b4(1), f4(0), b4(1),
                  pl.BlockSpec((A_DIM, vl, LANES), lambda i: (0, 0, 0))],
        out_specs=[pl.BlockSpec((tb, vl, LANES), lambda i: (i, 0, 0)),
                   pl.BlockSpec((tb, vl, LANES), lambda i: (nb - 1 - i, 0, 0))],
        out_shape=[o_shape, o_shape],
        scratch_shapes=[pltpu.VMEM((A_DIM, vl, LANES), F32),
                        pltpu.VMEM((tb + 1, LANES, LANES), F32),
                        pltpu.VMEM((tb, LANES, LANES), F32),
                        pltpu.VMEM((tb, LANES, LANES), F32)],
        compiler_params=_cparams(("arbitrary",)),
    )(ra, ra, wk, wk, bv, bv, s0)


def _gla_kernel(pb_ref, wgk_ref, bgk_ref, ng_ref, s0_ref, yb_ref, sfin_ref, s_scr, o_scr, *, has_init):
    t_len = pb_ref.shape[0]
    n_chunks = t_len // GLA_CHUNK
    c_len = GLA_CHUNK
    qw = B_HEADS * B_KEY
    vw = B_HEADS * B_VAL
    ri = lax.broadcasted_iota(jnp.int32, (c_len, c_len), 0)
    ci = lax.broadcasted_iota(jnp.int32, (c_len, c_len), 1)
    tri = (ri >= ci, ri <= ci)
    lane = lax.broadcasted_iota(jnp.int32, (1, LANES), 1)
    half_mask = (lane < B_KEY, lane >= B_KEY)
    ones_cl = jnp.ones((c_len, LANES), F32)

    s_scr[...] = jnp.zeros_like(s_scr)
    if has_init:
        for d in range(2):
            for h in range(B_HEADS):
                s_scr[d, h, pl.ds((h % 2) * B_KEY, B_KEY), :] = s0_ref[0, 0, d, h]

    def chunk(c, d):
        rows = pl.ds(pl.multiple_of(c * c_len, c_len), c_len)
        q = pb_ref[rows, 0:qw] * (B_KEY ** -0.5)
        k = pb_ref[rows, qw:2 * qw]
        gdt = pb_ref[rows, 2 * qw + 2 * vw:2 * qw + 2 * vw + LANES]
        gk = _log_sigmoid(_dot(gdt, wgk_ref[d], HI) + bgk_ref[d:d + 1, :]) / GLA_GATE_NORM
        b = _dot(tri[d].astype(F32), gk, HI)
        outs = []
        for p in range(B_HEADS // 2):
            sl = slice(p * LANES, (p + 1) * LANES)
            bp = b[:, sl]
            tot_row = bp[c_len - 1:c_len, :] if d == 0 else bp[0:1, :]
            tot_col = lax.dot_general(gk[:, sl], ones_cl, (((0,), (0,)), ((), ())),
                                      precision=HI, preferred_element_type=F32)
            qe = q[:, sl] * jnp.exp(bp)
            ke = (k[:, sl] * jnp.exp(-bp)).astype(BF16)
            kd = k[:, sl] * jnp.exp(tot_row - bp)
            decay_col = jnp.exp(tot_col)
            for hh in range(2):
                h = 2 * p + hh
                v = pb_ref[rows, 2 * qw + h * B_VAL:2 * qw + (h + 1) * B_VAL].astype(BF16)
                qm = jnp.where(half_mask[hh], qe, 0.0).astype(BF16)
                km = jnp.where(half_mask[hh], kd, 0.0).astype(BF16)
                att = lax.dot_general(qm, ke, (((1,), (1,)), ((), ())), preferred_element_type=F32)
                att = jnp.where(tri[d], att, 0.0).astype(BF16)
                s_old = s_scr[d, h]
                o = _dot(att, v) + _dot(qm, s_old.astype(BF16))
                kv = lax.dot_general(km, v, (((0,), (0,)), ((), ())), preferred_element_type=F32)
                s_scr[d, h] = decay_col * s_old + kv
                outs.append(o)
        return rows, outs

    def fwd_body(c, carry):
        rows, outs = chunk(c, 0)
        for h in range(B_HEADS):
            o_scr[rows, h * B_VAL:(h + 1) * B_VAL] = outs[h]
        return carry

    lax.fori_loop(0, n_chunks, fwd_body, 0)

    def bwd_body(cc, carry):
        c = n_chunks - 1 - cc
        rows, outs = chunk(c, 1)
        for h in range(B_HEADS):
            sl = slice(h * B_VAL, (h + 1) * B_VAL)
            o = o_scr[rows, sl] + outs[h]
            o = o * lax.rsqrt(jnp.mean(o * o, axis=-1, keepdims=True) + EPS) * ng_ref[...]
            gate = pb_ref[rows, 2 * qw + vw + h * B_VAL:2 * qw + vw + (h + 1) * B_VAL]
            yb_ref[rows, sl] = o * (gate * jax.nn.sigmoid(gate))
        return carry

    lax.fori_loop(0, n_chunks, bwd_body, 0)

    for d in range(2):
        for h in range(B_HEADS):
            sfin_ref[0, 0, d, h] = s_scr[d, h, pl.ds((h % 2) * B_KEY, B_KEY), :]


def _gla(pb, row0, n_seq, t_len, wgk, bgk, ng, s0, has_init):
    cols = pb.shape[1]
    blk0 = row0 // t_len
    st_shape = (n_seq, 1, 2, B_HEADS, B_KEY, B_VAL)
    st_spec = pl.BlockSpec((1, 1, 2, B_HEADS, B_KEY, B_VAL), lambda i: (i, 0, 0, 0, 0, 0))
    full = lambda w: pl.BlockSpec(w.shape, lambda i: (0,) * w.ndim)
    return pl.pallas_call(
        functools.partial(_gla_kernel, has_init=has_init),
        grid=(n_seq,),
        in_specs=[pl.BlockSpec((t_len, cols), lambda i: (i + blk0, 0)), full(wgk), full(bgk), full(ng), st_spec],
        out_specs=[pl.BlockSpec((t_len, B_HEADS * B_VAL), lambda i: (i, 0)), st_spec],
        out_shape=[jax.ShapeDtypeStruct((n_seq * t_len, B_HEADS * B_VAL), F32),
                   jax.ShapeDtypeStruct(st_shape, F32)],
        scratch_shapes=[pltpu.VMEM((2, B_HEADS, LANES, B_VAL), F32),
                        pltpu.VMEM((t_len, B_HEADS * B_VAL), F32)],
        compiler_params=_cparams(("parallel",)),
    )(pb, wgk, bgk, ng, s0)


def _mix_kernel(of_ref, ob_ref, bonus_ref, g_ref, yb_ref, pg_ref, x_ref, g1_ref, sh2_ref, sc2_ref,
                lng_ref, lnb_ref, seg_ref, woa_ref, wob_ref, wo_ref, n2g_ref, wr_ref, br_ref,
                x1_ref, h2_ref, gates_ref):
    d_model = x_ref.shape[1]
    seg = seg_ref[...] * (1.0 / A_DIM)
    o = of_ref[...] + ob_ref[...]
    mu = _dot(o, seg, HI)
    dev = o - mu
    var = _dot(dev * dev, seg, HI)
    on = dev * lax.rsqrt(var + GN_EPS) * lng_ref[...] + lnb_ref[...]
    ya = ((on + bonus_ref[...]) * g_ref[...]).astype(BF16)
    ta = _dot(ya, woa_ref[...])
    tb = _dot(yb_ref[...].astype(BF16), wob_ref[...])
    pg = pg_ref[...]
    u = jax.nn.sigmoid(pg[:, :d_model]) * ta + jax.nn.sigmoid(pg[:, d_model:]) * tb
    mix = _dot(u.astype(BF16), wo_ref[...])
    x1 = x_ref[...] + g1_ref[0] * mix
    x1_ref[...] = x1
    y = x1 * lax.rsqrt(jnp.mean(x1 * x1, axis=-1, keepdims=True) + EPS)
    h2 = (y * n2g_ref[...]) * (1.0 + sc2_ref[0]) + sh2_ref[0]
    h2_ref[...] = h2.astype(BF16)

    logits = _dot(h2, wr_ref[...], HI) + br_ref[...]
    lane = lax.broadcasted_iota(jnp.int32, logits.shape, 1)
    neg = jnp.float32(-jnp.inf)
    big = jnp.int32(1 << 20)
    is_g = lane < N_GROUPS
    gl = jnp.where(is_g, logits, neg)
    gmax = jnp.max(gl, axis=-1, keepdims=True)
    grp = jnp.min(jnp.where(gl == gmax, lane, big), axis=-1, keepdims=True)
    g_w = 1.0 / jnp.sum(jnp.where(is_g, jnp.exp(gl - gmax), 0.0), axis=-1, keepdims=True)
    lo = N_GROUPS + grp * N_PER_GROUP
    in_grp = (lane >= lo) & (lane < lo + N_PER_GROUP)
    el = jnp.where(in_grp, logits, neg)
    m1 = jnp.max(el, axis=-1, keepdims=True)
    i1 = jnp.min(jnp.where(el == m1, lane, big), axis=-1, keepdims=True)
    el2 = jnp.where(lane == i1, neg, el)
    m2 = jnp.max(el2, axis=-1, keepdims=True)
    i2 = jnp.min(jnp.where(el2 == m2, lane, big), axis=-1, keepdims=True)
    e2 = jnp.exp(m2 - m1)
    w1 = g_w / (1.0 + e2)
    w2 = g_w * e2 / (1.0 + e2)
    gates = jnp.where(lane == i1, w1, 0.0) + jnp.where(lane == i2, w2, 0.0)
    gates_ref[...] = pltpu.roll(gates, LANES - N_GROUPS, axis=1)


def _mix(of, ob, bonus, g, yb, pg, x, mod3, consts, n_ctx, t_lat):
    n, d = x.shape
    tm = 256
    row = functools.partial(_mod_row, tm=tm, n_ctx=n_ctx, t_lat=t_lat)
    full = lambda w: pl.BlockSpec(w.shape, lambda i: (0,) * w.ndim)
    tile = lambda a: pl.BlockSpec((tm, a.shape[1]), lambda i: (i, 0))
    mod_spec = lambda c: pl.BlockSpec((1, 1, d), lambda i: (row(i), 0, c))
    return pl.pallas_call(
        _mix_kernel,
        grid=(n // tm,),
        in_specs=[tile(of), tile(ob), tile(bonus), tile(g), tile(yb), tile(pg), tile(x),
                  mod_spec(2), mod_spec(3), mod_spec(4)] + [full(w) for w in consts],
        out_specs=[pl.BlockSpec((tm, d), lambda i: (i, 0)), pl.BlockSpec((tm, d), lambda i: (i, 0)),
                   pl.BlockSpec((tm, LANES), lambda i: (i, 0))],
        out_shape=[jax.ShapeDtypeStruct((n, d), F32), jax.ShapeDtypeStruct((n, d), BF16),
                   jax.ShapeDtypeStruct((n, LANES), F32)],
        compiler_params=_cparams(("parallel",)),
    )(of, ob, bonus, g, yb, pg, x, mod3, mod3, mod3, *consts)


def _moe_kernel(h_ref, gates_ref, wg_ref, wu_ref, wd_ref, x1_ref, g2_ref, fng_ref, y_ref, acc_ref):
    e = pl.program_id(1)

    @pl.when(e == 0)
    def _():
        acc_ref[...] = jnp.zeros_like(acc_ref)

    gates = gates_ref[...]
    lane = lax.broadcasted_iota(jnp.int32, gates.shape, 1)
    gcol = jnp.sum(jnp.where(lane == e, gates, 0.0), axis=-1, keepdims=True)

    @pl.when(jnp.max(gcol) > 0.0)
    def _():
        h = h_ref[...]
        hg = _dot(h, wg_ref[0])
        hu = _dot(h, wu_ref[0])
        act = (hg * jax.nn.sigmoid(hg)) * hu * gcol
        acc_ref[...] += _dot(act.astype(BF16), wd_ref[0])

    @pl.when(e == pl.num_programs(1) - 1)
    def _():
        x2 = x1_ref[...] + g2_ref[0] * acc_ref[...]
        y = x2 * lax.rsqrt(jnp.mean(x2 * x2, axis=-1, keepdims=True) + EPS)
        y_ref[...] = y * fng_ref[...]


def _moe(h2, gates, wg, wu, wd, x1, mod3, fng, n_ctx, t_lat):
    n, d = x1.shape
    n_exp, _, hid = wg.shape
    tm = 1024
    row = functools.partial(_mod_row, tm=tm, n_ctx=n_ctx, t_lat=t_lat)
    return pl.pallas_call(
        _moe_kernel,
        grid=(n // tm, n_exp),
        in_specs=[pl.BlockSpec((tm, d), lambda i, e: (i, 0)),
                  pl.BlockSpec((tm, LANES), lambda i, e: (i, 0)),
                  pl.BlockSpec((1, d, hid), lambda i, e: (e, 0, 0)),
                  pl.BlockSpec((1, d, hid), lambda i, e: (e, 0, 0)),
                  pl.BlockSpec((1, hid, d), lambda i, e: (e, 0, 0)),
                  pl.BlockSpec((tm, d), lambda i, e: (i, 0)),
                  pl.BlockSpec((1, 1, d), lambda i, e: (row(i), 0, 5)),
                  pl.BlockSpec((1, d), lambda i, e: (0, 0))],
        out_specs=pl.BlockSpec((tm, d), lambda i, e: (i, 0)),
        out_shape=jax.ShapeDtypeStruct((n, d), F32),
        scratch_shapes=[pltpu.VMEM((tm, d), F32)],
        compiler_params=_cparams(("parallel", "arbitrary")),
    )(h2, gates, wg, wu, wd, x1, mod3, fng)


def _block_diag2(a, b):
    za = jnp.zeros((a.shape[0], b.shape[1]), a.dtype)
    zb = jnp.zeros((b.shape[0], a.shape[1]), a.dtype)
    return jnp.concatenate([jnp.concatenate([a, za], 1), jnp.concatenate([zb, b], 1)], 0)


def kernel(x_prompt, x_sample, state_rwkv, state_gla, c, c_ctx, w_mod, b_mod, norm1_g, norm2_g, w_in, shift_mu, a_w0, a_w_up, a_a0, a_a_up, a_g_up, a_k_k, a_k_a, a_r_k, a_ln_g, a_ln_b, b_gk_up, b_gk_bias, b_norm_g, w_out_a, w_out_b, w_o, w_router_group, b_router_group, w_router_expert, b_router_expert, w_exp_gate, w_exp_up, w_exp_down, final_norm_g):
    n_ctx_seq, t_ctx, d = x_prompt.shape
    n_lat_seq, t_lat, _ = x_sample.shape
    assert w_mod.shape[0] == 1, "single-layer configuration"
    n_ctx = n_ctx_seq * t_ctx
    n_lat = n_lat_seq * t_lat
    x = jnp.concatenate([x_prompt.reshape(n_ctx, d), x_sample.reshape(n_lat, d)], 0)

    cvec = jnp.concatenate([c_ctx[None], c, jnp.zeros((SUBLANES - 1 - n_lat_seq, d), F32)], 0)
    mod = _modulation(cvec, w_mod[0], b_mod[0][None])
    mod3 = mod[:, None, :]

    a_cols = shift_mu.shape[1]
    b_cols = 2 * B_HEADS * B_KEY + 2 * B_HEADS * B_VAL + 2 * b_gk_up.shape[2]
    b_pad = -b_cols % LANES
    w_in0 = w_in[0].astype(BF16)
    wa = w_in0[:, :a_cols]
    wb = jnp.pad(w_in0[:, a_cols:a_cols + b_cols], ((0, 0), (0, b_pad)))
    wg = w_in0[:, a_cols + b_cols:]
    pa, pb, pg = _in_proj(x, mod3, norm1_g, wa, wb, wg, n_ctx, t_lat)

    seg = jnp.kron(jnp.eye(A_HEADS, dtype=F32), jnp.ones((A_DIM, A_DIM), F32))
    consts_a = (shift_mu, _block_diag2(a_w_up[0, 0], a_w_up[0, 1]), _block_diag2(a_a_up[0, 0], a_a_up[0, 1]),
                a_g_up[0], a_w0[0], a_a0[0], a_k_k, a_k_a, a_r_k[0].reshape(1, A_WIDTH), seg)
    ra_c, wkf_c, bvf_c, wkb_c, bvb_c, bonus_c, g_c = _prep_a(pa, 0, n_ctx_seq, t_ctx, False, consts_a)
    ra_l, wkf_l, bvf_l, wkb_l, bvb_l, bonus_l, g_l = _prep_a(pa, n_ctx, n_lat_seq, t_lat, True, consts_a)

    hr = lambda a, ns, t: a.reshape(ns, t * A_HEADS, LANES)
    o_ctx, sfin = _scan_ctx(hr(ra_c, n_ctx_seq, t_ctx),
                            jnp.stack([hr(wkf_c, n_ctx_seq, t_ctx), hr(wkb_c, n_ctx_seq, t_ctx)]),
                            jnp.stack([hr(bvf_c, n_ctx_seq, t_ctx), hr(bvb_c, n_ctx_seq, t_ctx)]),
                            n_ctx_seq, t_ctx, 16)
    o_ctx = o_ctx.reshape(2, t_ctx, A_DIM, n_ctx_seq, A_HEADS).transpose(0, 3, 1, 4, 2).reshape(2, n_ctx, A_WIDTH)
    new_state_rwkv = sfin.reshape(2, A_DIM, A_DIM, n_ctx_seq, A_HEADS).transpose(3, 0, 4, 2, 1)[:, None]

    vl = A_DIM // 2
    s0 = state_rwkv[:, 0].reshape(n_lat_seq, 2, A_HEADS, 2, vl, A_DIM)
    s0 = s0.transpose(5, 4, 3, 1, 0, 2).reshape(A_DIM, vl, LANES)
    of_l, ob_l = _scan_lat(hr(ra_l, n_lat_seq, t_lat),
                           jnp.stack([hr(wkf_l, n_lat_seq, t_lat), hr(wkb_l, n_lat_seq, t_lat)]),
                           jnp.stack([hr(bvf_l, n_lat_seq, t_lat), hr(bvb_l, n_lat_seq, t_lat)]),
                           s0, n_lat_seq, t_lat, 16)

    def lat_rows(o, dirn):
        o = o.reshape(t_lat, vl, 2, 2, n_lat_seq, A_HEADS)[:, :, :, dirn]
        return o.transpose(3, 0, 4, 2, 1).reshape(n_lat, A_WIDTH)

    o_f = jnp.concatenate([o_ctx[0], lat_rows(of_l, 0)], 0)
    o_b = jnp.concatenate([o_ctx[1], lat_rows(ob_l, 1)], 0)
    bonus = jnp.concatenate([bonus_c, bonus_l], 0)
    g_a = jnp.concatenate([g_c, g_l], 0)

    rank = b_gk_up.shape[2]
    wgk = jnp.zeros((2, LANES, B_HEADS * B_KEY), F32)
    wgk = wgk.at[0, :rank].set(b_gk_up[0, 0]).at[1, rank:2 * rank].set(b_gk_up[0, 1])
    zero_gla = jnp.zeros((n_ctx_seq, 1, 2, B_HEADS, B_KEY, B_VAL), F32)
    yb_c, new_state_gla = _gla(pb, 0, n_ctx_seq, t_ctx, wgk, b_gk_bias[0], b_norm_g, zero_gla, False)
    yb_l, _ = _gla(pb, n_ctx, n_lat_seq, t_lat, wgk, b_gk_bias[0], b_norm_g, state_gla, True)
    yb = jnp.concatenate([yb_c, yb_l], 0)

    w_r = jnp.concatenate([w_router_group[0], w_router_expert[0].transpose(1, 0, 2).reshape(d, N_EXPERTS)], 1)
    w_r = jnp.pad(w_r, ((0, 0), (0, LANES - w_r.shape[1])))
    b_r = jnp.concatenate([b_router_group[0], b_router_expert[0].reshape(N_EXPERTS)])
    b_r = jnp.pad(b_r, (0, LANES - b_r.shape[0]))[None]
    consts_m = (a_ln_g, a_ln_b, seg, w_out_a[0].astype(BF16), w_out_b[0].astype(BF16), w_o[0].astype(BF16),
                norm2_g, w_r, b_r)
    x1, h2, gates = _mix(o_f, o_b, bonus, g_a, yb, pg, x, mod3, consts_m, n_ctx, t_lat)

    y = _moe(h2, gates, w_exp_gate[0].astype(BF16), w_exp_up[0].astype(BF16), w_exp_down[0].astype(BF16),
             x1, mod3, final_norm_g[None], n_ctx, t_lat)
    return (y[:n_ctx].reshape(n_ctx_seq, t_ctx, d), y[n_ctx:].reshape(n_lat_seq, t_lat, d),
            new_state_rwkv, new_state_gla)
```

```python
import functools

import jax
import jax.numpy as jnp
from jax import lax
from jax.experimental import pallas as pl
from jax.experimental.pallas import tpu as pltpu

F32 = jnp.float32
BF16 = jnp.bfloat16
HI = lax.Precision.HIGHEST

LANES = 128
SUBLANES = 8
VMEM_LIMIT = 56 * 1024 * 1024

EPS = 1e-6
GN_EPS = 64e-5
GRID_W = 64
A_HEADS, A_DIM = 8, 64
A_WIDTH = A_HEADS * A_DIM
B_HEADS, B_KEY, B_VAL = 4, 64, 128
GLA_CHUNK = 64
GLA_GATE_NORM = 16.0
N_GROUPS, N_PER_GROUP = 4, 8
N_EXPERTS = N_GROUPS * N_PER_GROUP


def _cparams(sem):
    return pltpu.CompilerParams(dimension_semantics=sem, vmem_limit_bytes=VMEM_LIMIT)


def _log_sigmoid(z):
    return jnp.minimum(z, 0.0) - jnp.log(1.0 + jnp.exp(-jnp.abs(z)))


def _dot(a, b, precision=None):
    return jnp.dot(a, b, preferred_element_type=F32, precision=precision)


def _mod_kernel(c_ref, w_ref, b_ref, o_ref):
    c = c_ref[...]
    o_ref[...] = _dot(c * jax.nn.sigmoid(c), w_ref[...], HI) + b_ref[...]


def _modulation(cvec, w_mod, b_mod):
    rows, d = cvec.shape
    n = w_mod.shape[1]
    tn = n // 4
    return pl.pallas_call(
        _mod_kernel,
        grid=(n // tn,),
        in_specs=[pl.BlockSpec((rows, d), lambda j: (0, 0)),
                  pl.BlockSpec((d, tn), lambda j: (0, j)),
                  pl.BlockSpec((1, tn), lambda j: (0, j))],
        out_specs=pl.BlockSpec((rows, tn), lambda j: (0, j)),
        out_shape=jax.ShapeDtypeStruct((rows, n), F32),
        compiler_params=_cparams(("arbitrary",)),
    )(cvec, w_mod, b_mod)


def _mod_row(i, tm, n_ctx, t_lat):
    start = i * tm
    return jnp.where(start < n_ctx, 0, 1 + (start - n_ctx) // t_lat)


def _in_proj_kernel(x_ref, sh_ref, sc_ref, g_ref, wa_ref, wb_ref, wg_ref, pa_ref, pb_ref, pg_ref):
    x = x_ref[...]
    y = x * lax.rsqrt(jnp.mean(x * x, axis=-1, keepdims=True) + EPS)
    h = ((y * g_ref[...]) * (1.0 + sc_ref[0]) + sh_ref[0]).astype(BF16)
    pa_ref[...] = _dot(h, wa_ref[...])
    pb_ref[...] = _dot(h, wb_ref[...])
    pg_ref[...] = _dot(h, wg_ref[...])


def _in_proj(x, mod3, norm_g, wa, wb, wg, n_ctx, t_lat):
    n, d = x.shape
    tm = 256
    row = functools.partial(_mod_row, tm=tm, n_ctx=n_ctx, t_lat=t_lat)
    full = lambda w: pl.BlockSpec(w.shape, lambda i: (0, 0))
    outs = [jax.ShapeDtypeStruct((n, w.shape[1]), F32) for w in (wa, wb, wg)]
    return pl.pallas_call(
        _in_proj_kernel,
        grid=(n // tm,),
        in_specs=[pl.BlockSpec((tm, d), lambda i: (i, 0)),
                  pl.BlockSpec((1, 1, d), lambda i: (row(i), 0, 0)),
                  pl.BlockSpec((1, 1, d), lambda i: (row(i), 0, 1)),
                  pl.BlockSpec((1, d), lambda i: (0, 0)),
                  full(wa), full(wb), full(wg)],
        out_specs=[pl.BlockSpec((tm, w.shape[1]), lambda i: (i, 0)) for w in (wa, wb, wg)],
        out_shape=outs,
        compiler_params=_cparams(("parallel",)),
    )(x, mod3, mod3, norm_g, wa, wb, wg)


def _pair_heads(x, y):
    lane = lax.broadcasted_iota(jnp.int32, (1, LANES), 1)
    low = lane < A_DIM
    tiles = []
    for p in range(A_HEADS // 2):
        xv = x[:, p * LANES:(p + 1) * LANES]
        yv = y[:, p * LANES:(p + 1) * LANES]
        tiles.append(jnp.where(low, xv, pltpu.roll(yv, A_DIM, axis=1)))
        tiles.append(jnp.where(low, pltpu.roll(xv, A_DIM, axis=1), yv))
    return tiles


def _prep_a_kernel(pa_ref, prev_ref, next_ref, mu_ref, wup_ref, aup_ref, gup_ref, w0_ref, a0_ref, kk_ref, ka_ref,
                   rk_ref, seg_ref, ra_ref, wkf_ref, bvf_ref, wkb_ref, bvb_ref, bonus_ref, g_ref, *,
                   on_grid, seq_len):
    pa = pa_ref[...]
    t_len, cols = pa.shape
    halo = prev_ref.shape[0]
    ext = jnp.concatenate([prev_ref[...], pa, next_ref[...]], axis=0)
    col = lax.broadcasted_iota(jnp.int32, (1, cols), 1)
    row = lax.broadcasted_iota(jnp.int32, (t_len, 1), 0) + pl.program_id(1) * t_len
    quarter = col // (cols // 4)

    def shifted(s):
        if s % SUBLANES == 0:
            moved = ext[halo - s:halo - s + t_len]
        else:
            moved = pltpu.roll(ext, s % ext.shape[0], axis=0)[halo:halo + t_len]
        valid = (row >= s) & (row < seq_len + s)
        return jnp.where(valid, moved, 0.0)

    if on_grid:
        gcol = row % GRID_W
        left = jnp.where(gcol != 0, shifted(1), 0.0)
        right = jnp.where(gcol != GRID_W - 1, shifted(-1), 0.0)
        nb = jnp.where(quarter == 0, left,
                       jnp.where(quarter == 1, right,
                                 jnp.where(quarter == 2, shifted(GRID_W), shifted(-GRID_W))))
    else:
        nb = jnp.where(quarter % 2 == 0, shifted(1), shifted(-1))
    pa = pa + (nb - pa) * mu_ref[...]

    r = pa[:, 0:A_WIDTH]
    k = pa[:, A_WIDTH:2 * A_WIDTH]
    v = pa[:, 2 * A_WIDTH:3 * A_WIDTH]
    c0 = 3 * A_WIDTH
    wd = pa[:, c0:c0 + LANES]
    ad = pa[:, c0 + LANES:c0 + 2 * LANES]
    gd = pa[:, c0 + 2 * LANES:c0 + 3 * LANES]
    seg = seg_ref[...]

    up_w = _dot(jnp.tanh(wd), wup_ref[...], HI)
    up_a = _dot(ad, aup_ref[...], HI)
    g_ref[...] = _dot(jax.nn.sigmoid(gd), gup_ref[...], HI)

    kk = k * kk_ref[...]
    kk = kk / jnp.maximum(jnp.sqrt(_dot(kk * kk, seg, HI)), 1e-12)
    neg_kk = -kk
    kd_sum = None
    wk_refs = (wkf_ref, wkb_ref)
    bv_refs = (bvf_ref, bvb_ref)
    for d in range(2):
        sl = slice(d * A_WIDTH, (d + 1) * A_WIDTH)
        w_log = _log_sigmoid(w0_ref[d:d + 1, :] + up_w[:, sl]) - 0.5
        decay = jnp.exp(-jnp.exp(w_log))
        a = jax.nn.sigmoid(a0_ref[d:d + 1, :] + up_a[:, sl])
        kd = k * (1.0 + (a - 1.0) * ka_ref[...])
        kd_sum = kd if kd_sum is None else kd_sum + kd
        for h, tile in enumerate(_pair_heads(decay, kd)):
            wk_refs[d][pl.ds(h, t_len, stride=A_HEADS), :] = tile
        for h, tile in enumerate(_pair_heads(kk * a, v)):
            bv_refs[d][pl.ds(h, t_len, stride=A_HEADS), :] = tile
    for h, tile in enumerate(_pair_heads(r, neg_kk)):
        ra_ref[pl.ds(h, t_len, stride=A_HEADS), :] = tile
    bonus_ref[...] = _dot(r * kd_sum * rk_ref[...], seg, HI) * v


def _prep_a(pa, row0, n_seq, t_len, on_grid, consts):
    n_rows, cols = pa.shape
    tt = 256
    halo = GRID_W if on_grid else SUBLANES
    tiles = t_len // tt
    blk0 = row0 // tt
    main = lambda i, j: (blk0 + i * tiles + j, 0)
    prev = lambda i, j: (jnp.maximum((blk0 + i * tiles + j) * (tt // halo) - 1, 0), 0)
    nxt = lambda i, j: (jnp.minimum((blk0 + i * tiles + j + 1) * (tt // halo), n_rows // halo - 1), 0)
    full = lambda w: pl.BlockSpec(w.shape, lambda i, j: (0,) * w.ndim)
    hr = jax.ShapeDtypeStruct((n_seq * t_len * A_HEADS, LANES), F32)
    tok = jax.ShapeDtypeStruct((n_seq * t_len, A_WIDTH), F32)
    hr_spec = pl.BlockSpec((tt * A_HEADS, LANES), lambda i, j: (i * tiles + j, 0))
    tok_spec = pl.BlockSpec((tt, A_WIDTH), lambda i, j: (i * tiles + j, 0))
    return pl.pallas_call(
        functools.partial(_prep_a_kernel, on_grid=on_grid, seq_len=t_len),
        grid=(n_seq, tiles),
        in_specs=[pl.BlockSpec((tt, cols), main), pl.BlockSpec((halo, cols), prev),
                  pl.BlockSpec((halo, cols), nxt)] + [full(w) for w in consts],
        out_specs=[hr_spec] * 5 + [tok_spec] * 2,
        out_shape=[hr] * 5 + [tok] * 2,
        compiler_params=_cparams(("parallel", "parallel")),
    )(pa, pa, pa, *consts)


def _scan_block(load_tile, store_o, s_ref, kra, kwk, kbv, tb, v_rows, lane_rep):
    def transpose_step(j, carry):
        kra[j] = load_tile(0, j).T
        kwk[j] = load_tile(1, j).T
        kbv[j] = load_tile(2, j).T
        return carry

    lax.fori_loop(0, tb, transpose_step, 0)
    kra[tb] = kra[tb - 1]

    def row_b(buf, j, r):
        return jnp.broadcast_to(buf[j, pl.ds(r, 1), :], (v_rows, LANES))

    n_kb = A_DIM // SUBLANES

    def sa_init(kb, acc):
        for kk in range(SUBLANES):
            k = kb * SUBLANES + kk
            acc = acc + s_ref[k] * row_b(kra, 0, A_DIM + k)
        return acc

    sa0 = lax.fori_loop(0, n_kb, sa_init, jnp.zeros((v_rows, LANES), F32))
    lane = lax.broadcasted_iota(jnp.int32, (1, LANES), 1)

    def step(j, sa):
        vt = kbv[j, pl.ds(A_DIM, A_DIM), :]
        if lane_rep == 1:
            v = vt
        else:
            v = jnp.where(lane < LANES // 2, vt[:v_rows], vt[v_rows:])

        def kblock(kb, carry):
            o, sa_next = carry
            for kk in range(SUBLANES):
                k = kb * SUBLANES + kk
                s_new = (s_ref[k] * row_b(kwk, j, k) + row_b(kwk, j, A_DIM + k) * v
                         + row_b(kbv, j, k) * sa)
                s_ref[k] = s_new
                o = o + s_new * row_b(kra, j, k)
                sa_next = sa_next + s_new * row_b(kra, j + 1, A_DIM + k)
            return o, sa_next

        zero = jnp.zeros((v_rows, LANES), F32)
        o, sa_next = lax.fori_loop(0, n_kb, kblock, (zero, zero))
        store_o(j, o)
        return sa_next

    lax.fori_loop(0, tb, step, sa0)


def _scan_ctx_kernel(ra_ref, wk_ref, bv_ref, o_ref, sfin_ref, s_ref, kra, kwk, kbv, *, tb, n_seq):
    g = pl.program_id(0)
    i = pl.program_id(1)

    @pl.when(i == 0)
    def _():
        s_ref[...] = jnp.zeros_like(s_ref)

    def local_t(j):
        return j + g * (tb - 1 - 2 * j)

    refs = (ra_ref, wk_ref, bv_ref)

    def load_tile(which, j):
        ref = refs[which]
        rows = pl.ds(pl.multiple_of(local_t(j) * A_HEADS, A_HEADS), A_HEADS)
        slabs = [ref[b, rows, :] if which == 0 else ref[0, b, rows, :] for b in range(n_seq)]
        return jnp.concatenate(slabs, axis=0)

    def store_o(j, o):
        o_ref[0, local_t(j)] = o

    _scan_block(load_tile, store_o, s_ref, kra, kwk, kbv, tb, A_DIM, 1)

    @pl.when(i == pl.num_programs(1) - 1)
    def _():
        sfin_ref[0] = s_ref[...]


def _scan_ctx(ra, wk, bv, n_seq, t_len, tb):
    nb = t_len // tb
    blk = lambda g, i: jnp.where(g == 0, i, nb - 1 - i)
    rows = tb * A_HEADS
    return pl.pallas_call(
        functools.partial(_scan_ctx_kernel, tb=tb, n_seq=n_seq),
        grid=(2, nb),
        in_specs=[pl.BlockSpec((n_seq, rows, LANES), lambda g, i: (0, blk(g, i), 0)),
                  pl.BlockSpec((1, n_seq, rows, LANES), lambda g, i: (g, 0, blk(g, i), 0)),
                  pl.BlockSpec((1, n_seq, rows, LANES), lambda g, i: (g, 0, blk(g, i), 0))],
        out_specs=[pl.BlockSpec((1, tb, A_DIM, LANES), lambda g, i: (g, blk(g, i), 0, 0)),
                   pl.BlockSpec((1, A_DIM, A_DIM, LANES), lambda g, i: (g, 0, 0, 0))],
        out_shape=[jax.ShapeDtypeStruct((2, t_len, A_DIM, LANES), F32),
                   jax.ShapeDtypeStruct((2, A_DIM, A_DIM, LANES), F32)],
        scratch_shapes=[pltpu.VMEM((A_DIM, A_DIM, LANES), F32),
                        pltpu.VMEM((tb + 1, LANES, LANES), F32),
                        pltpu.VMEM((tb, LANES, LANES), F32),
                        pltpu.VMEM((tb, LANES, LANES), F32)],
        compiler_params=_cparams(("arbitrary", "arbitrary")),
    )(ra, wk, bv)


def _scan_lat_kernel(raf_ref, rab_ref, wkf_ref, wkb_ref, bvf_ref, bvb_ref, s0_ref, of_ref, ob_ref,
                     s_ref, kra, kwk, kbv, *, tb, n_seq):
    i = pl.program_id(0)

    @pl.when(i == 0)
    def _():
        s_ref[...] = s0_ref[...]

    refs = ((raf_ref, rab_ref), (wkf_ref, wkb_ref), (bvf_ref, bvb_ref))

    def load_tile(which, j):
        fwd, bwd = refs[which]
        rows_f = pl.ds(pl.multiple_of(j * A_HEADS, A_HEADS), A_HEADS)
        rows_b = pl.ds(pl.multiple_of((tb - 1 - j) * A_HEADS, A_HEADS), A_HEADS)
        if which == 0:
            half = ([fwd[b, rows_f, :] for b in range(n_seq)] + [bwd[b, rows_b, :] for b in range(n_seq)])
        else:
            half = ([fwd[0, b, rows_f, :] for b in range(n_seq)] + [bwd[0, b, rows_b, :] for b in range(n_seq)])
        return jnp.concatenate(half + half, axis=0)

    def store_o(j, o):
        of_ref[j] = o
        ob_ref[tb - 1 - j] = o

    _scan_block(load_tile, store_o, s_ref, kra, kwk, kbv, tb, A_DIM // 2, 2)


def _scan_lat(ra, wk, bv, s0, n_seq, t_len, tb):
    nb = t_len // tb
    rows = tb * A_HEADS
    vl = A_DIM // 2
    f3 = pl.BlockSpec((n_seq, rows, LANES), lambda i: (0, i, 0))
    b3 = pl.BlockSpec((n_seq, rows, LANES), lambda i: (0, nb - 1 - i, 0))
    f4 = lambda d: pl.BlockSpec((1, n_seq, rows, LANES), lambda i: (d, 0, i, 0))
    b4 = lambda d: pl.BlockSpec((1, n_seq, rows, LANES), lambda i: (d, 0, nb - 1 - i, 0))
    o_shape = jax.ShapeDtypeStruct((t_len, vl, LANES), F32)
    return pl.pallas_call(
        functools.partial(_scan_lat_kernel, tb=tb, n_seq=n_seq),
        grid=(nb,),
        in_specs=[f3, b3, f4(0), b4(1), f4(0), b4(1),
                  pl.BlockSpec((A_DIM, vl, LANES), lambda i: (0, 0, 0))],
        out_specs=[pl.BlockSpec((tb, vl, LANES), lambda i: (i, 0, 0)),
                   pl.BlockSpec((tb, vl, LANES), lambda i: (nb - 1 - i, 0, 0))],
        out_shape=[o_shape, o_shape],
        scratch_shapes=[pltpu.VMEM((A_DIM, vl, LANES), F32),
                        pltpu.VMEM((tb + 1, LANES, LANES), F32),
                        pltpu.VMEM((tb, LANES, LANES), F32),
                        pltpu.VMEM((tb, LANES, LANES), F32)],
        compiler_params=_cparams(("arbitrary",)),
    )(ra, ra, wk, wk, bv, bv, s0)


def _gla_kernel(pb_ref, wgk_ref, bgk_ref, ng_ref, s0_ref, yb_ref, sfin_ref, s_scr, o_scr, *, has_init):
    t_len = pb_ref.shape[0]
    n_chunks = t_len // GLA_CHUNK
    c_len = GLA_CHUNK
    qw = B_HEADS * B_KEY
    vw = B_HEADS * B_VAL
    ri = lax.broadcasted_iota(jnp.int32, (c_len, c_len), 0)
    ci = lax.broadcasted_iota(jnp.int32, (c_len, c_len), 1)
    tri = (ri >= ci, ri <= ci)
    lane = lax.broadcasted_iota(jnp.int32, (1, LANES), 1)
    half_mask = (lane < B_KEY, lane >= B_KEY)
    ones_cl = jnp.ones((c_len, LANES), F32)

    s_scr[...] = jnp.zeros_like(s_scr)
    if has_init:
        for d in range(2):
            for h in range(B_HEADS):
                s_scr[d, h, pl.ds((h % 2) * B_KEY, B_KEY), :] = s0_ref[0, 0, d, h]

    def chunk(c, d):
        rows = pl.ds(pl.multiple_of(c * c_len, c_len), c_len)
        q = pb_ref[rows, 0:qw] * (B_KEY ** -0.5)
        k = pb_ref[rows, qw:2 * qw]
        gdt = pb_ref[rows, 2 * qw + 2 * vw:2 * qw + 2 * vw + LANES]
        gk = _log_sigmoid(_dot(gdt, wgk_ref[d], HI) + bgk_ref[d:d + 1, :]) / GLA_GATE_NORM
        b = _dot(tri[d].astype(F32), gk, HI)
        outs = []
        for p in range(B_HEADS // 2):
            sl = slice(p * LANES, (p + 1) * LANES)
            bp = b[:, sl]
            tot_row = bp[c_len - 1:c_len, :] if d == 0 else bp[0:1, :]
            tot_col = lax.dot_general(gk[:, sl], ones_cl, (((0,), (0,)), ((), ())),
                                      precision=HI, preferred_element_type=F32)
            qe = q[:, sl] * jnp.exp(bp)
            ke = (k[:, sl] * jnp.exp(-bp)).astype(BF16)
            kd = k[:, sl] * jnp.exp(tot_row - bp)
            decay_col = jnp.exp(tot_col)
            for hh in range(2):
                h = 2 * p + hh
                v = pb_ref[rows, 2 * qw + h * B_VAL:2 * qw + (h + 1) * B_VAL].astype(BF16)
                qm = jnp.where(half_mask[hh], qe, 0.0).astype(BF16)
                km = jnp.where(half_mask[hh], kd, 0.0).astype(BF16)
                att = lax.dot_general(qm, ke, (((1,), (1,)), ((), ())), preferred_element_type=F32)
                att = jnp.where(tri[d], att, 0.0).astype(BF16)
                s_old = s_scr[d, h]
                o = _dot(att, v) + _dot(qm, s_old.astype(BF16))
                kv = lax.dot_general(km, v, (((0,), (0,)), ((), ())), preferred_element_type=F32)
                s_scr[d, h] = decay_col * s_old + kv
                outs.append(o)
        return rows, outs

    def fwd_body(c, carry):
        rows, outs = chunk(c, 0)
        for h in range(B_HEADS):
            o_scr[rows, h * B_VAL:(h + 1) * B_VAL] = outs[h]
        return carry

    lax.fori_loop(0, n_chunks, fwd_body, 0)

    def bwd_body(cc, carry):
        c = n_chunks - 1 - cc
        rows, outs = chunk(c, 1)
        for h in range(B_HEADS):
            sl = slice(h * B_VAL, (h + 1) * B_VAL)
            o = o_scr[rows, sl] + outs[h]
            o = o * lax.rsqrt(jnp.mean(o * o, axis=-1, keepdims=True) + EPS) * ng_ref[...]
            gate = pb_ref[rows, 2 * qw + vw + h * B_VAL:2 * qw + vw + (h + 1) * B_VAL]
            yb_ref[rows, sl] = o * (gate * jax.nn.sigmoid(gate))
        return carry

    lax.fori_loop(0, n_chunks, bwd_body, 0)

    for d in range(2):
        for h in range(B_HEADS):
            sfin_ref[0, 0, d, h] = s_scr[d, h, pl.ds((h % 2) * B_KEY, B_KEY), :]


def _gla(pb, row0, n_seq, t_len, wgk, bgk, ng, s0, has_init):
    cols = pb.shape[1]
    blk0 = row0 // t_len
    st_shape = (n_seq, 1, 2, B_HEADS, B_KEY, B_VAL)
    st_spec = pl.BlockSpec((1, 1, 2, B_HEADS, B_KEY, B_VAL), lambda i: (i, 0, 0, 0, 0, 0))
    full = lambda w: pl.BlockSpec(w.shape, lambda i: (0,) * w.ndim)
    return pl.pallas_call(
        functools.partial(_gla_kernel, has_init=has_init),
        grid=(n_seq,),
        in_specs=[pl.BlockSpec((t_len, cols), lambda i: (i + blk0, 0)), full(wgk), full(bgk), full(ng), st_spec],
        out_specs=[pl.BlockSpec((t_len, B_HEADS * B_VAL), lambda i: (i, 0)), st_spec],
        out_shape=[jax.ShapeDtypeStruct((n_seq * t_len, B_HEADS * B_VAL), F32),
                   jax.ShapeDtypeStruct(st_shape, F32)],
        scratch_shapes=[pltpu.VMEM((2, B_HEADS, LANES, B_VAL), F32),
                        pltpu.VMEM((t_len, B_HEADS * B_VAL), F32)],
        compiler_params=_cparams(("parallel",)),
    )(pb, wgk, bgk, ng, s0)


def _mix_kernel(of_ref, ob_ref, bonus_ref, g_ref, yb_ref, pg_ref, x_ref, g1_ref, sh2_ref, sc2_ref,
                lng_ref, lnb_ref, seg_ref, woa_ref, wob_ref, wo_ref, n2g_ref, wr_ref, br_ref,
                x1_ref, h2_ref, route_ref):
    d_model = x_ref.shape[1]
    seg = seg_ref[...] * (1.0 / A_DIM)
    o = of_ref[...] + ob_ref[...]
    mu = _dot(o, seg, HI)
    dev = o - mu
    var = _dot(dev * dev, seg, HI)
    on = dev * lax.rsqrt(var + GN_EPS) * lng_ref[...] + lnb_ref[...]
    ya = ((on + bonus_ref[...]) * g_ref[...]).astype(BF16)
    ta = _dot(ya, woa_ref[...])
    tb = _dot(yb_ref[...].astype(BF16), wob_ref[...])
    pg = pg_ref[...]
    u = jax.nn.sigmoid(pg[:, :d_model]) * ta + jax.nn.sigmoid(pg[:, d_model:]) * tb
    mix = _dot(u.astype(BF16), wo_ref[...])
    x1 = x_ref[...] + g1_ref[0] * mix
    x1_ref[...] = x1
    y = x1 * lax.rsqrt(jnp.mean(x1 * x1, axis=-1, keepdims=True) + EPS)
    h2 = (y * n2g_ref[...]) * (1.0 + sc2_ref[0]) + sh2_ref[0]
    for s in range(d_model // LANES):
        h2_ref[pl.ds(s, h2.shape[0], stride=SUBLANES), :] = h2[:, s * LANES:(s + 1) * LANES]

    logits = _dot(h2, wr_ref[...], HI) + br_ref[...]
    lane = lax.broadcasted_iota(jnp.int32, logits.shape, 1)
    neg = jnp.float32(-jnp.inf)
    big = jnp.int32(1 << 20)
    is_g = lane < N_GROUPS
    gl = jnp.where(is_g, logits, neg)
    gmax = jnp.max(gl, axis=-1, keepdims=True)
    grp = jnp.min(jnp.where(gl == gmax, lane, big), axis=-1, keepdims=True)
    g_w = 1.0 / jnp.sum(jnp.where(is_g, jnp.exp(gl - gmax), 0.0), axis=-1, keepdims=True)
    lo = N_GROUPS + grp * N_PER_GROUP
    in_grp = (lane >= lo) & (lane < lo + N_PER_GROUP)
    el = jnp.where(in_grp, logits, neg)
    m1 = jnp.max(el, axis=-1, keepdims=True)
    i1 = jnp.min(jnp.where(el == m1, lane, big), axis=-1, keepdims=True)
    el2 = jnp.where(lane == i1, neg, el)
    m2 = jnp.max(el2, axis=-1, keepdims=True)
    i2 = jnp.min(jnp.where(el2 == m2, lane, big), axis=-1, keepdims=True)
    e2 = jnp.exp(m2 - m1)
    w1 = g_w / (1.0 + e2)
    w2 = g_w * e2 / (1.0 + e2)
    e1 = (i1 - N_GROUPS).astype(F32)
    e2_id = (i2 - N_GROUPS).astype(F32)
    route_ref[...] = jnp.where(lane == 0, e1, jnp.where(lane == 1, e2_id, jnp.where(lane == 2, w1, w2)))


def _mix(of, ob, bonus, g, yb, pg, x, mod3, consts, n_ctx, t_lat):
    n, d = x.shape
    tm = 256
    row = functools.partial(_mod_row, tm=tm, n_ctx=n_ctx, t_lat=t_lat)
    full = lambda w: pl.BlockSpec(w.shape, lambda i: (0,) * w.ndim)
    tile = lambda a: pl.BlockSpec((tm, a.shape[1]), lambda i: (i, 0))
    mod_spec = lambda c: pl.BlockSpec((1, 1, d), lambda i: (row(i), 0, c))
    return pl.pallas_call(
        _mix_kernel,
        grid=(n // tm,),
        in_specs=[tile(of), tile(ob), tile(bonus), tile(g), tile(yb), tile(pg), tile(x),
                  mod_spec(2), mod_spec(3), mod_spec(4)] + [full(w) for w in consts],
        out_specs=[pl.BlockSpec((tm, d), lambda i: (i, 0)),
                   pl.BlockSpec((tm * d // LANES, LANES), lambda i: (i, 0)),
                   pl.BlockSpec((tm, LANES), lambda i: (i, 0))],
        out_shape=[jax.ShapeDtypeStruct((n, d), F32), jax.ShapeDtypeStruct((n * d // LANES, LANES), F32),
                   jax.ShapeDtypeStruct((n, LANES), F32)],
        compiler_params=_cparams(("parallel",)),
    )(of, ob, bonus, g, yb, pg, x, mod3, mod3, mod3, *consts)


MOE_TM = 256
MOE_TMF = 256


def _route_tables(route, n_tok):
    e = route[:, 0:2].astype(jnp.int32).reshape(-1)
    w = route[:, 2:4].reshape(-1)
    tok = jnp.arange(2 * n_tok, dtype=jnp.int32) // 2
    onehot = (e[:, None] == jnp.arange(N_EXPERTS, dtype=jnp.int32)[None]).astype(jnp.int32)
    csum = jnp.cumsum(onehot, axis=0)
    rank = jnp.take_along_axis(csum, e[:, None], axis=1)[:, 0] - 1
    tiles = (csum[-1] + MOE_TM - 1) // MOE_TM
    tile_end = jnp.cumsum(tiles)
    pos = (tile_end - tiles)[e] * MOE_TM + rank
    t_max = 2 * n_tok // MOE_TM + N_EXPERTS
    row_tok = jnp.full((t_max * MOE_TM,), n_tok, jnp.int32).at[pos].set(tok)
    row_w = jnp.zeros((t_max * MOE_TM,), F32).at[pos].set(w)
    tile_e = jnp.sum((tile_end[None, :] <= jnp.arange(t_max, dtype=jnp.int32)[:, None]).astype(jnp.int32), axis=1)
    tile_e = jnp.minimum(tile_e, N_EXPERTS - 1)
    return row_tok.reshape(t_max, 1, MOE_TM), row_w.reshape(t_max * MOE_TM, 1), tile_e, tile_end[-1:]


def _expert_kernel(tile_e_ref, n_tiles_ref, row_tok_ref, h2t_ref, row_w_ref, wg_ref, wu_ref, wd_ref,
                   y_ref, buf_ref, *, n_tok):
    t = pl.program_id(0)
    tm = row_tok_ref.shape[2]
    n_sub = buf_ref.shape[0] // tm

    @pl.when(t < n_tiles_ref[0])
    def _():
        def gather(i, carry):
            tok = jnp.minimum(row_tok_ref[0, 0, i], n_tok - 1)
            src = pl.ds(pl.multiple_of(tok * SUBLANES, SUBLANES), SUBLANES)
            buf_ref[pl.ds(pl.multiple_of(i * SUBLANES, SUBLANES), SUBLANES), :] = h2t_ref[src, :]
            return carry

        lax.fori_loop(0, tm, gather, 0, unroll=8)
        h = jnp.concatenate([buf_ref[pl.ds(s, tm, stride=SUBLANES), :] for s in range(n_sub)], axis=1).astype(BF16)
        hg = _dot(h, wg_ref[0])
        hu = _dot(h, wu_ref[0])
        act = (hg * jax.nn.sigmoid(hg)) * hu * row_w_ref[...]
        y = _dot(act.astype(BF16), wd_ref[0])
        for s in range(n_sub):
            y_ref[pl.ds(s, tm, stride=SUBLANES), :] = y[:, s * LANES:(s + 1) * LANES]

    @pl.when(t >= n_tiles_ref[0])
    def _():
        y_ref[...] = jnp.zeros_like(y_ref)


def _experts(h2t, row_tok, row_w, tile_e, n_tiles, wg, wu, wd, n_tok):
    t_max, _, tm = row_tok.shape
    n_exp, d, hid = wg.shape
    n_sub = d // LANES
    grid_spec = pltpu.PrefetchScalarGridSpec(
        num_scalar_prefetch=2,
        grid=(t_max,),
        in_specs=[pl.BlockSpec((1, 1, tm), lambda t, te, nt: (t, 0, 0), memory_space=pltpu.SMEM),
                  pl.BlockSpec(h2t.shape, lambda t, te, nt: (0, 0), pipeline_mode=pl.Buffered(1)),
                  pl.BlockSpec((tm, 1), lambda t, te, nt: (t, 0)),
                  pl.BlockSpec((1, d, hid), lambda t, te, nt: (te[t], 0, 0)),
                  pl.BlockSpec((1, d, hid), lambda t, te, nt: (te[t], 0, 0)),
                  pl.BlockSpec((1, hid, d), lambda t, te, nt: (te[t], 0, 0))],
        out_specs=pl.BlockSpec((tm * n_sub, LANES), lambda t, te, nt: (t, 0)),
        scratch_shapes=[pltpu.VMEM((tm * n_sub, LANES), F32)],
    )
    return pl.pallas_call(
        functools.partial(_expert_kernel, n_tok=n_tok),
        grid_spec=grid_spec,
        out_shape=jax.ShapeDtypeStruct((t_max * tm * n_sub, LANES), F32),
        compiler_params=_cparams(("arbitrary",)),
    )(tile_e, n_tiles, row_tok, h2t, row_w, wg, wu, wd)


def _combine_kernel(row_tok_ref, ys_ref, x1_ref, g2_ref, fng_ref, y_ref, acc_ref, *, t_max):
    step = pl.program_id(0)
    tm = row_tok_ref.shape[2]
    tmf = x1_ref.shape[0]
    n_sub = x1_ref.shape[1] // LANES
    group = SUBLANES

    @pl.when(step == 0)
    def _():
        acc_ref[...] = jnp.zeros_like(acc_ref)

    @pl.when(step < t_max)
    def _():
        def body(ib, carry):
            slots, vals = [], []
            for k in range(group):
                i = ib * group + k
                tok = row_tok_ref[0, 0, i]
                slot = (tok // tmf, pl.ds(pl.multiple_of((tok % tmf) * n_sub, n_sub), n_sub))
                slots.append(slot)
                vals.append(acc_ref[slot[0], slot[1], :]
                            + ys_ref[pl.ds(pl.multiple_of(i * n_sub, n_sub), n_sub), :])
            for slot, val in zip(slots, vals):
                acc_ref[slot[0], slot[1], :] = val
            return carry

        lax.fori_loop(0, tm // group, body, 0)

    @pl.when(step >= t_max)
    def _():
        j = step - t_max
        moe = jnp.concatenate([acc_ref[j, pl.ds(s, tmf, stride=n_sub), :] for s in range(n_sub)], axis=1)
        x2 = x1_ref[...] + g2_ref[0] * moe
        y = x2 * lax.rsqrt(jnp.mean(x2 * x2, axis=-1, keepdims=True) + EPS)
        y_ref[...] = y * fng_ref[...]


def _combine(ys, row_tok, x1, mod3, fng, n_ctx, t_lat):
    n, d = x1.shape
    t_max, _, tm = row_tok.shape
    tmf = MOE_TMF
    n_sub = d // LANES
    assert n_sub == SUBLANES, "one token must fill exactly one (8, 128) tile"
    row = functools.partial(_mod_row, tm=tmf, n_ctx=n_ctx, t_lat=t_lat)
    src = lambda s: jnp.minimum(s, t_max - 1)
    dst = lambda s: jnp.maximum(s - t_max, 0)
    return pl.pallas_call(
        functools.partial(_combine_kernel, t_max=t_max),
        grid=(t_max + n // tmf,),
        in_specs=[pl.BlockSpec((1, 1, tm), lambda s: (src(s), 0, 0), memory_space=pltpu.SMEM),
                  pl.BlockSpec((tm * n_sub, LANES), lambda s: (src(s), 0)),
                  pl.BlockSpec((tmf, d), lambda s: (dst(s), 0)),
                  pl.BlockSpec((1, 1, d), lambda s: (row(dst(s)), 0, 5)),
                  pl.BlockSpec((1, d), lambda s: (0, 0))],
        out_specs=pl.BlockSpec((tmf, d), lambda s: (dst(s), 0)),
        out_shape=jax.ShapeDtypeStruct((n, d), F32),
        scratch_shapes=[pltpu.VMEM((n // tmf + 1, tmf * n_sub, LANES), F32)],
        compiler_params=_cparams(("arbitrary",)),
    )(row_tok, ys, x1, mod3, fng)


def _block_diag2(a, b):
    za = jnp.zeros((a.shape[0], b.shape[1]), a.dtype)
    zb = jnp.zeros((b.shape[0], a.shape[1]), a.dtype)
    return jnp.concatenate([jnp.concatenate([a, za], 1), jnp.concatenate([zb, b], 1)], 0)


def kernel(x_prompt, x_sample, state_rwkv, state_gla, c, c_ctx, w_mod, b_mod, norm1_g, norm2_g, w_in, shift_mu, a_w0, a_w_up, a_a0, a_a_up, a_g_up, a_k_k, a_k_a, a_r_k, a_ln_g, a_ln_b, b_gk_up, b_gk_bias, b_norm_g, w_out_a, w_out_b, w_o, w_router_group, b_router_group, w_router_expert, b_router_expert, w_exp_gate, w_exp_up, w_exp_down, final_norm_g):
    n_ctx_seq, t_ctx, d = x_prompt.shape
    n_lat_seq, t_lat, _ = x_sample.shape
    assert w_mod.shape[0] == 1, "single-layer configuration"
    n_ctx = n_ctx_seq * t_ctx
    n_lat = n_lat_seq * t_lat
    x = jnp.concatenate([x_prompt.reshape(n_ctx, d), x_sample.reshape(n_lat, d)], 0)

    cvec = jnp.concatenate([c_ctx[None], c, jnp.zeros((SUBLANES - 1 - n_lat_seq, d), F32)], 0)
    mod = _modulation(cvec, w_mod[0], b_mod[0][None])
    mod3 = mod[:, None, :]

    a_cols = shift_mu.shape[1]
    b_cols = 2 * B_HEADS * B_KEY + 2 * B_HEADS * B_VAL + 2 * b_gk_up.shape[2]
    b_pad = -b_cols % LANES
    w_in0 = w_in[0].astype(BF16)
    wa = w_in0[:, :a_cols]
    wb = jnp.pad(w_in0[:, a_cols:a_cols + b_cols], ((0, 0), (0, b_pad)))
    wg = w_in0[:, a_cols + b_cols:]
    pa, pb, pg = _in_proj(x, mod3, norm1_g, wa, wb, wg, n_ctx, t_lat)

    seg = jnp.kron(jnp.eye(A_HEADS, dtype=F32), jnp.ones((A_DIM, A_DIM), F32))
    consts_a = (shift_mu, _block_diag2(a_w_up[0, 0], a_w_up[0, 1]), _block_diag2(a_a_up[0, 0], a_a_up[0, 1]),
                a_g_up[0], a_w0[0], a_a0[0], a_k_k, a_k_a, a_r_k[0].reshape(1, A_WIDTH), seg)
    ra_c, wkf_c, bvf_c, wkb_c, bvb_c, bonus_c, g_c = _prep_a(pa, 0, n_ctx_seq, t_ctx, False, consts_a)
    ra_l, wkf_l, bvf_l, wkb_l, bvb_l, bonus_l, g_l = _prep_a(pa, n_ctx, n_lat_seq, t_lat, True, consts_a)

    hr = lambda a, ns, t: a.reshape(ns, t * A_HEADS, LANES)
    o_ctx, sfin = _scan_ctx(hr(ra_c, n_ctx_seq, t_ctx),
                            jnp.stack([hr(wkf_c, n_ctx_seq, t_ctx), hr(wkb_c, n_ctx_seq, t_ctx)]),
                            jnp.stack([hr(bvf_c, n_ctx_seq, t_ctx), hr(bvb_c, n_ctx_seq, t_ctx)]),
                            n_ctx_seq, t_ctx, 16)
    o_ctx = o_ctx.reshape(2, t_ctx, A_DIM, n_ctx_seq, A_HEADS).transpose(0, 3, 1, 4, 2).reshape(2, n_ctx, A_WIDTH)
    new_state_rwkv = sfin.reshape(2, A_DIM, A_DIM, n_ctx_seq, A_HEADS).transpose(3, 0, 4, 2, 1)[:, None]

    vl = A_DIM // 2
    s0 = state_rwkv[:, 0].reshape(n_lat_seq, 2, A_HEADS, 2, vl, A_DIM)
    s0 = s0.transpose(5, 4, 3, 1, 0, 2).reshape(A_DIM, vl, LANES)
    of_l, ob_l = _scan_lat(hr(ra_l, n_lat_seq, t_lat),
                           jnp.stack([hr(wkf_l, n_lat_seq, t_lat), hr(wkb_l, n_lat_seq, t_lat)]),
                           jnp.stack([hr(bvf_l, n_lat_seq, t_lat), hr(bvb_l, n_lat_seq, t_lat)]),
                           s0, n_lat_seq, t_lat, 16)

    def lat_rows(o, dirn):
        o = o.reshape(t_lat, vl, 2, 2, n_lat_seq, A_HEADS)[:, :, :, dirn]
        return o.transpose(3, 0, 4, 2, 1).reshape(n_lat, A_WIDTH)

    o_f = jnp.concatenate([o_ctx[0], lat_rows(of_l, 0)], 0)
    o_b = jnp.concatenate([o_ctx[1], lat_rows(ob_l, 1)], 0)
    bonus = jnp.concatenate([bonus_c, bonus_l], 0)
    g_a = jnp.concatenate([g_c, g_l], 0)

    rank = b_gk_up.shape[2]
    wgk = jnp.zeros((2, LANES, B_HEADS * B_KEY), F32)
    wgk = wgk.at[0, :rank].set(b_gk_up[0, 0]).at[1, rank:2 * rank].set(b_gk_up[0, 1])
    zero_gla = jnp.zeros((n_ctx_seq, 1, 2, B_HEADS, B_KEY, B_VAL), F32)
    yb_c, new_state_gla = _gla(pb, 0, n_ctx_seq, t_ctx, wgk, b_gk_bias[0], b_norm_g, zero_gla, False)
    yb_l, _ = _gla(pb, n_ctx, n_lat_seq, t_lat, wgk, b_gk_bias[0], b_norm_g, state_gla, True)
    yb = jnp.concatenate([yb_c, yb_l], 0)

    w_r = jnp.concatenate([w_router_group[0], w_router_expert[0].transpose(1, 0, 2).reshape(d, N_EXPERTS)], 1)
    w_r = jnp.pad(w_r, ((0, 0), (0, LANES - w_r.shape[1])))
    b_r = jnp.concatenate([b_router_group[0], b_router_expert[0].reshape(N_EXPERTS)])
    b_r = jnp.pad(b_r, (0, LANES - b_r.shape[0]))[None]
    consts_m = (a_ln_g, a_ln_b, seg, w_out_a[0].astype(BF16), w_out_b[0].astype(BF16), w_o[0].astype(BF16),
                norm2_g, w_r, b_r)
    x1, h2t, route = _mix(o_f, o_b, bonus, g_a, yb, pg, x, mod3, consts_m, n_ctx, t_lat)

    n_tok = n_ctx + n_lat
    row_tok, row_w, tile_e, n_tiles = _route_tables(route, n_tok)
    ys = _experts(h2t, row_tok, row_w, tile_e, n_tiles, w_exp_gate[0].astype(BF16), w_exp_up[0].astype(BF16),
                  w_exp_down[0].astype(BF16), n_tok)
    y = _combine(ys, row_tok, x1, mod3, final_norm_g[None], n_ctx, t_lat)
    return (y[:n_ctx].reshape(n_ctx_seq, t_ctx, d), y[n_ctx:].reshape(n_lat_seq, t_lat, d),
            new_state_rwkv, new_state_gla)
```

```python
import functools

import jax
import jax.numpy as jnp
from jax import lax
from jax.experimental import pallas as pl
from jax.experimental.pallas import tpu as pltpu

F32 = jnp.float32
BF16 = jnp.bfloat16
HI = lax.Precision.HIGHEST

LANES = 128
SUBLANES = 8
VMEM_LIMIT = 56 * 1024 * 1024

EPS = 1e-6
GN_EPS = 64e-5
GRID_W = 64
A_HEADS, A_DIM = 8, 64
A_WIDTH = A_HEADS * A_DIM
B_HEADS, B_KEY, B_VAL = 4, 64, 128
GLA_CHUNK = 64
GLA_GATE_NORM = 16.0
N_GROUPS, N_PER_GROUP = 4, 8
N_EXPERTS = N_GROUPS * N_PER_GROUP


def _cparams(sem):
    return pltpu.CompilerParams(dimension_semantics=sem, vmem_limit_bytes=VMEM_LIMIT)


def _log_sigmoid(z):
    return jnp.minimum(z, 0.0) - jnp.log(1.0 + jnp.exp(-jnp.abs(z)))


def _dot(a, b, precision=None):
    return jnp.dot(a, b, preferred_element_type=F32, precision=precision)


def _mod_kernel(c_ref, w_ref, b_ref, o_ref):
    c = c_ref[...]
    o_ref[...] = _dot(c * jax.nn.sigmoid(c), w_ref[...], HI) + b_ref[...]


def _modulation(cvec, w_mod, b_mod):
    rows, d = cvec.shape
    n = w_mod.shape[1]
    tn = n // 4
    return pl.pallas_call(
        _mod_kernel,
        grid=(n // tn,),
        in_specs=[pl.BlockSpec((rows, d), lambda j: (0, 0)),
                  pl.BlockSpec((d, tn), lambda j: (0, j)),
                  pl.BlockSpec((1, tn), lambda j: (0, j))],
        out_specs=pl.BlockSpec((rows, tn), lambda j: (0, j)),
        out_shape=jax.ShapeDtypeStruct((rows, n), F32),
        compiler_params=_cparams(("arbitrary",)),
    )(cvec, w_mod, b_mod)


def _mod_row(i, tm, n_ctx, t_lat):
    start = i * tm
    return jnp.where(start < n_ctx, 0, 1 + (start - n_ctx) // t_lat)


def _in_proj_kernel(x_ref, sh_ref, sc_ref, g_ref, wa_ref, wb_ref, wg_ref, pa_ref, pb_ref, pg_ref):
    x = x_ref[...]
    y = x * lax.rsqrt(jnp.mean(x * x, axis=-1, keepdims=True) + EPS)
    h = ((y * g_ref[...]) * (1.0 + sc_ref[0]) + sh_ref[0]).astype(BF16)
    pa_ref[...] = _dot(h, wa_ref[...])
    pb_ref[...] = _dot(h, wb_ref[...])
    pg_ref[...] = _dot(h, wg_ref[...])


def _in_proj(x, mod3, norm_g, wa, wb, wg, n_ctx, t_lat):
    n, d = x.shape
    tm = 256
    row = functools.partial(_mod_row, tm=tm, n_ctx=n_ctx, t_lat=t_lat)
    full = lambda w: pl.BlockSpec(w.shape, lambda i: (0, 0))
    outs = [jax.ShapeDtypeStruct((n, w.shape[1]), F32) for w in (wa, wb, wg)]
    return pl.pallas_call(
        _in_proj_kernel,
        grid=(n // tm,),
        in_specs=[pl.BlockSpec((tm, d), lambda i: (i, 0)),
                  pl.BlockSpec((1, 1, d), lambda i: (row(i), 0, 0)),
                  pl.BlockSpec((1, 1, d), lambda i: (row(i), 0, 1)),
                  pl.BlockSpec((1, d), lambda i: (0, 0)),
                  full(wa), full(wb), full(wg)],
        out_specs=[pl.BlockSpec((tm, w.shape[1]), lambda i: (i, 0)) for w in (wa, wb, wg)],
        out_shape=outs,
        compiler_params=_cparams(("parallel",)),
    )(x, mod3, mod3, norm_g, wa, wb, wg)


def _pair_heads(x, y):
    lane = lax.broadcasted_iota(jnp.int32, (1, LANES), 1)
    low = lane < A_DIM
    tiles = []
    for p in range(A_HEADS // 2):
        xv = x[:, p * LANES:(p + 1) * LANES]
        yv = y[:, p * LANES:(p + 1) * LANES]
        tiles.append(jnp.where(low, xv, pltpu.roll(yv, A_DIM, axis=1)))
        tiles.append(jnp.where(low, pltpu.roll(xv, A_DIM, axis=1), yv))
    return tiles


def _prep_a_kernel(pa_ref, prev_ref, next_ref, mu_ref, wup_ref, aup_ref, gup_ref, w0_ref, a0_ref, kk_ref, ka_ref,
                   rk_ref, seg_ref, ra_ref, wk_ref, bv_ref, bonus_ref, g_ref, *, on_grid, seq_len):
    pa = pa_ref[...]
    t_len, cols = pa.shape
    halo = prev_ref.shape[0]
    ext = jnp.concatenate([prev_ref[...], pa, next_ref[...]], axis=0)
    col = lax.broadcasted_iota(jnp.int32, (1, cols), 1)
    row = lax.broadcasted_iota(jnp.int32, (t_len, 1), 0) + pl.program_id(1) * t_len
    quarter = col // (cols // 4)

    def shifted(s):
        if s % SUBLANES == 0:
            moved = ext[halo - s:halo - s + t_len]
        else:
            moved = pltpu.roll(ext, s % ext.shape[0], axis=0)[halo:halo + t_len]
        valid = (row >= s) & (row < seq_len + s)
        return jnp.where(valid, moved, 0.0)

    if on_grid:
        gcol = row % GRID_W
        left = jnp.where(gcol != 0, shifted(1), 0.0)
        right = jnp.where(gcol != GRID_W - 1, shifted(-1), 0.0)
        nb = jnp.where(quarter == 0, left,
                       jnp.where(quarter == 1, right,
                                 jnp.where(quarter == 2, shifted(GRID_W), shifted(-GRID_W))))
    else:
        nb = jnp.where(quarter % 2 == 0, shifted(1), shifted(-1))
    pa = pa + (nb - pa) * mu_ref[...]

    r = pa[:, 0:A_WIDTH]
    k = pa[:, A_WIDTH:2 * A_WIDTH]
    v = pa[:, 2 * A_WIDTH:3 * A_WIDTH]
    c0 = 3 * A_WIDTH
    wd = pa[:, c0:c0 + LANES]
    ad = pa[:, c0 + LANES:c0 + 2 * LANES]
    gd = pa[:, c0 + 2 * LANES:c0 + 3 * LANES]
    seg = seg_ref[...]

    up_w = _dot(jnp.tanh(wd), wup_ref[...], HI)
    up_a = _dot(ad, aup_ref[...], HI)
    g_ref[...] = _dot(jax.nn.sigmoid(gd), gup_ref[...], HI)

    kk = k * kk_ref[...]
    kk = kk / jnp.maximum(jnp.sqrt(_dot(kk * kk, seg, HI)), 1e-12)
    neg_kk = -kk
    kd_sum = None
    for d in range(2):
        sl = slice(d * A_WIDTH, (d + 1) * A_WIDTH)
        w_log = _log_sigmoid(w0_ref[d:d + 1, :] + up_w[:, sl]) - 0.5
        decay = jnp.exp(-jnp.exp(w_log))
        a = jax.nn.sigmoid(a0_ref[d:d + 1, :] + up_a[:, sl])
        kd = k * (1.0 + (a - 1.0) * ka_ref[...])
        kd_sum = kd if kd_sum is None else kd_sum + kd
        for h, tile in enumerate(_pair_heads(decay, kd)):
            wk_ref[d, pl.ds(h, t_len, stride=A_HEADS), :] = tile
        for h, tile in enumerate(_pair_heads(kk * a, v)):
            bv_ref[d, pl.ds(h, t_len, stride=A_HEADS), :] = tile
    for h, tile in enumerate(_pair_heads(r, neg_kk)):
        ra_ref[pl.ds(h, t_len, stride=A_HEADS), :] = tile
    bonus_ref[...] = _dot(r * kd_sum * rk_ref[...], seg, HI) * v


def _skip_refs(fn, start, count):
    return lambda *refs, **kw: fn(*refs[:start], *refs[start + count:], **kw)


def _prep_a(pa, row0, n_seq, t_len, on_grid, consts, prev_outs=None):
    n_rows, cols = pa.shape
    tt = 256
    halo = GRID_W if on_grid else SUBLANES
    tiles = t_len // tt
    blk0 = row0 // tt
    main = lambda i, j: (blk0 + i * tiles + j, 0)
    prev = lambda i, j: (jnp.maximum((blk0 + i * tiles + j) * (tt // halo) - 1, 0), 0)
    nxt = lambda i, j: (jnp.minimum((blk0 + i * tiles + j + 1) * (tt // halo), n_rows // halo - 1), 0)
    full = lambda w: pl.BlockSpec(w.shape, lambda i, j: (0,) * w.ndim)
    hr = jax.ShapeDtypeStruct((n_rows * A_HEADS, LANES), F32)
    hr2 = jax.ShapeDtypeStruct((2, n_rows * A_HEADS, LANES), F32)
    tok = jax.ShapeDtypeStruct((n_rows, A_WIDTH), F32)
    hr_spec = pl.BlockSpec((tt * A_HEADS, LANES), main)
    hr2_spec = pl.BlockSpec((2, tt * A_HEADS, LANES), lambda i, j: (0,) + main(i, j))
    tok_spec = pl.BlockSpec((tt, A_WIDTH), main)
    kern = functools.partial(_prep_a_kernel, on_grid=on_grid, seq_len=t_len)
    in_specs = [pl.BlockSpec((tt, cols), main), pl.BlockSpec((halo, cols), prev),
                pl.BlockSpec((halo, cols), nxt)] + [full(w) for w in consts]
    args = [pa, pa, pa, *consts]
    aliases = {}
    if prev_outs is not None:
        n_in = len(args)
        kern = _skip_refs(kern, n_in, len(prev_outs))
        aliases = {n_in + k: k for k in range(len(prev_outs))}
        in_specs += [pl.BlockSpec(memory_space=pl.ANY)] * len(prev_outs)
        args += list(prev_outs)
    return pl.pallas_call(
        kern,
        grid=(n_seq, tiles),
        in_specs=in_specs,
        out_specs=[hr_spec, hr2_spec, hr2_spec, tok_spec, tok_spec],
        out_shape=[hr, hr2, hr2, tok, tok],
        input_output_aliases=aliases,
        compiler_params=_cparams(("parallel", "parallel")),
    )(*args)


def _scan_block(load_tile, store_o, s_ref, kra, kwk, kbv, tb, v_rows, lane_rep):
    def transpose_step(j, carry):
        kra[j] = load_tile(0, j).T
        kwk[j] = load_tile(1, j).T
        kbv[j] = load_tile(2, j).T
        return carry

    lax.fori_loop(0, tb, transpose_step, 0)
    kra[tb] = kra[tb - 1]

    def row_b(buf, j, r):
        return jnp.broadcast_to(buf[j, pl.ds(r, 1), :], (v_rows, LANES))

    n_kb = A_DIM // SUBLANES

    def sa_init(kb, acc):
        for kk in range(SUBLANES):
            k = kb * SUBLANES + kk
            acc = acc + s_ref[k] * row_b(kra, 0, A_DIM + k)
        return acc

    sa0 = lax.fori_loop(0, n_kb, sa_init, jnp.zeros((v_rows, LANES), F32))
    lane = lax.broadcasted_iota(jnp.int32, (1, LANES), 1)

    def step(j, sa):
        vt = kbv[j, pl.ds(A_DIM, A_DIM), :]
        if lane_rep == 1:
            v = vt
        else:
            v = jnp.where(lane < LANES // 2, vt[:v_rows], vt[v_rows:])

        def kblock(kb, carry):
            o, sa_next = carry
            for kk in range(SUBLANES):
                k = kb * SUBLANES + kk
                s_new = (s_ref[k] * row_b(kwk, j, k) + row_b(kwk, j, A_DIM + k) * v
                         + row_b(kbv, j, k) * sa)
                s_ref[k] = s_new
                o = o + s_new * row_b(kra, j, k)
                sa_next = sa_next + s_new * row_b(kra, j + 1, A_DIM + k)
            return o, sa_next

        zero = jnp.zeros((v_rows, LANES), F32)
        o, sa_next = lax.fori_loop(0, n_kb, kblock, (zero, zero))
        store_o(j, o)
        return sa_next

    lax.fori_loop(0, tb, step, sa0)


def _scan_ctx_kernel(ra_ref, wk_ref, bv_ref, o_ref, sfin_ref, s_ref, kra, kwk, kbv, *, tb, n_seq):
    g = pl.program_id(0)
    i = pl.program_id(1)

    @pl.when(i == 0)
    def _():
        s_ref[...] = jnp.zeros_like(s_ref)

    def local_t(j):
        return j + g * (tb - 1 - 2 * j)

    refs = (ra_ref, wk_ref, bv_ref)

    def load_tile(which, j):
        ref = refs[which]
        rows = pl.ds(pl.multiple_of(local_t(j) * A_HEADS, A_HEADS), A_HEADS)
        slabs = [ref[b, rows, :] if which == 0 else ref[0, b, rows, :] for b in range(n_seq)]
        return jnp.concatenate(slabs, axis=0)

    def store_o(j, o):
        o_ref[0, local_t(j)] = o

    _scan_block(load_tile, store_o, s_ref, kra, kwk, kbv, tb, A_DIM, 1)

    @pl.when(i == pl.num_programs(1) - 1)
    def _():
        sfin_ref[0] = s_ref[...]


def _scan_ctx(ra, wk, bv, n_seq, t_len, tb):
    nb = t_len // tb
    blk = lambda g, i: jnp.where(g == 0, i, nb - 1 - i)
    rows = tb * A_HEADS
    return pl.pallas_call(
        functools.partial(_scan_ctx_kernel, tb=tb, n_seq=n_seq),
        grid=(2, nb),
        in_specs=[pl.BlockSpec((n_seq, rows, LANES), lambda g, i: (0, blk(g, i), 0)),
                  pl.BlockSpec((1, n_seq, rows, LANES), lambda g, i: (g, 0, blk(g, i), 0)),
                  pl.BlockSpec((1, n_seq, rows, LANES), lambda g, i: (g, 0, blk(g, i), 0))],
        out_specs=[pl.BlockSpec((1, tb, A_DIM, LANES), lambda g, i: (g, blk(g, i), 0, 0)),
                   pl.BlockSpec((1, A_DIM, A_DIM, LANES), lambda g, i: (g, 0, 0, 0))],
        out_shape=[jax.ShapeDtypeStruct((2, t_len, A_DIM, LANES), F32),
                   jax.ShapeDtypeStruct((2, A_DIM, A_DIM, LANES), F32)],
        scratch_shapes=[pltpu.VMEM((A_DIM, A_DIM, LANES), F32),
                        pltpu.VMEM((tb + 1, LANES, LANES), F32),
                        pltpu.VMEM((tb, LANES, LANES), F32),
                        pltpu.VMEM((tb, LANES, LANES), F32)],
        compiler_params=_cparams(("arbitrary", "arbitrary")),
    )(ra, wk, bv)


def _scan_lat_kernel(raf_ref, rab_ref, wkf_ref, wkb_ref, bvf_ref, bvb_ref, s0_ref, of_ref, ob_ref,
                     s_ref, kra, kwk, kbv, *, tb, n_seq):
    i = pl.program_id(0)

    @pl.when(i == 0)
    def _():
        s_ref[...] = s0_ref[...]

    refs = ((raf_ref, rab_ref), (wkf_ref, wkb_ref), (bvf_ref, bvb_ref))

    def load_tile(which, j):
        fwd, bwd = refs[which]
        rows_f = pl.ds(pl.multiple_of(j * A_HEADS, A_HEADS), A_HEADS)
        rows_b = pl.ds(pl.multiple_of((tb - 1 - j) * A_HEADS, A_HEADS), A_HEADS)
        if which == 0:
            half = ([fwd[b, rows_f, :] for b in range(n_seq)] + [bwd[b, rows_b, :] for b in range(n_seq)])
        else:
            half = ([fwd[0, b, rows_f, :] for b in range(n_seq)] + [bwd[0, b, rows_b, :] for b in range(n_seq)])
        return jnp.concatenate(half + half, axis=0)

    def store_o(j, o):
        of_ref[j] = o
        ob_ref[tb - 1 - j] = o

    _scan_block(load_tile, store_o, s_ref, kra, kwk, kbv, tb, A_DIM // 2, 2)


def _scan_lat(ra, wk, bv, s0, n_seq, t_len, tb, ub):
    nb = t_len // tb
    rows = tb * A_HEADS
    vl = A_DIM // 2
    f3 = pl.BlockSpec((n_seq, rows, LANES), lambda i: (ub, i, 0))
    b3 = pl.BlockSpec((n_seq, rows, LANES), lambda i: (ub, nb - 1 - i, 0))
    f4 = lambda d: pl.BlockSpec((1, n_seq, rows, LANES), lambda i: (d, ub, i, 0))
    b4 = lambda d: pl.BlockSpec((1, n_seq, rows, LANES), lambda i: (d, ub, nb - 1 - i, 0))
    o_shape = jax.ShapeDtypeStruct((t_len, vl, LANES), F32)
    return pl.pallas_call(
        functools.partial(_scan_lat_kernel, tb=tb, n_seq=n_seq),
        grid=(nb,),
        in_specs=[f3, b3, f4(0), b4(1), f4(0), b4(1),
                  pl.BlockSpec((A_DIM, vl, LANES), lambda i: (0, 0, 0))],
        out_specs=[pl.BlockSpec((tb, vl, LANES), lambda i: (i, 0, 0)),
                   pl.BlockSpec((tb, vl, LANES), lambda i: (nb - 1 - i, 0, 0))],
        out_shape=[o_shape, o_shape],
        scratch_shapes=[pltpu.VMEM((A_DIM, vl, LANES), F32),
                        pltpu.VMEM((tb + 1, LANES, LANES), F32),
                        pltpu.VMEM((tb, LANES, LANES), F32),
                        pltpu.VMEM((tb, LANES, LANES), F32)],
        compiler_params=_cparams(("arbitrary",)),
    )(ra, ra, wk, wk, bv, bv, s0)


def _gla_kernel(pb_ref, wgk_ref, bgk_ref, ng_ref, s0_ref, yb_ref, sfin_ref, s_scr, o_scr, *, has_init):
    t_len = pb_ref.shape[0]
    n_chunks = t_len // GLA_CHUNK
    c_len = GLA_CHUNK
    qw = B_HEADS * B_KEY
    vw = B_HEADS * B_VAL
    ri = lax.broadcasted_iota(jnp.int32, (c_len, c_len), 0)
    ci = lax.broadcasted_iota(jnp.int32, (c_len, c_len), 1)
    tri = (ri >= ci, ri <= ci)
    lane = lax.broadcasted_iota(jnp.int32, (1, LANES), 1)
    half_mask = (lane < B_KEY, lane >= B_KEY)
    ones_cl = jnp.ones((c_len, LANES), F32)

    s_scr[...] = jnp.zeros_like(s_scr)
    if has_init:
        for d in range(2):
            for h in range(B_HEADS):
                s_scr[d, h, pl.ds((h % 2) * B_KEY, B_KEY), :] = s0_ref[0, 0, d, h]

    def chunk(c, d):
        rows = pl.ds(pl.multiple_of(c * c_len, c_len), c_len)
        q = pb_ref[rows, 0:qw] * (B_KEY ** -0.5)
        k = pb_ref[rows, qw:2 * qw]
        gdt = pb_ref[rows, 2 * qw + 2 * vw:2 * qw + 2 * vw + LANES]
        gk = _log_sigmoid(_dot(gdt, wgk_ref[d], HI) + bgk_ref[d:d + 1, :]) / GLA_GATE_NORM
        b = _dot(tri[d].astype(F32), gk, HI)
        outs = []
        for p in range(B_HEADS // 2):
            sl = slice(p * LANES, (p + 1) * LANES)
            bp = b[:, sl]
            tot_row = bp[c_len - 1:c_len, :] if d == 0 else bp[0:1, :]
            tot_col = lax.dot_general(gk[:, sl], ones_cl, (((0,), (0,)), ((), ())),
                                      precision=HI, preferred_element_type=F32)
            qe = q[:, sl] * jnp.exp(bp)
            ke = (k[:, sl] * jnp.exp(-bp)).astype(BF16)
            kd = k[:, sl] * jnp.exp(tot_row - bp)
            decay_col = jnp.exp(tot_col)
            for hh in range(2):
                h = 2 * p + hh
                v = pb_ref[rows, 2 * qw + h * B_VAL:2 * qw + (h + 1) * B_VAL].astype(BF16)
                qm = jnp.where(half_mask[hh], qe, 0.0).astype(BF16)
                km = jnp.where(half_mask[hh], kd, 0.0).astype(BF16)
                att = lax.dot_general(qm, ke, (((1,), (1,)), ((), ())), preferred_element_type=F32)
                att = jnp.where(tri[d], att, 0.0).astype(BF16)
                s_old = s_scr[d, h]
                o = _dot(att, v) + _dot(qm, s_old.astype(BF16))
                kv = lax.dot_general(km, v, (((0,), (0,)), ((), ())), preferred_element_type=F32)
                s_scr[d, h] = decay_col * s_old + kv
                outs.append(o)
        return rows, outs

    def fwd_body(c, carry):
        rows, outs = chunk(c, 0)
        for h in range(B_HEADS):
            o_scr[rows, h * B_VAL:(h + 1) * B_VAL] = outs[h]
        return carry

    lax.fori_loop(0, n_chunks, fwd_body, 0)

    def bwd_body(cc, carry):
        c = n_chunks - 1 - cc
        rows, outs = chunk(c, 1)
        for h in range(B_HEADS):
            sl = slice(h * B_VAL, (h + 1) * B_VAL)
            o = o_scr[rows, sl] + outs[h]
            o = o * lax.rsqrt(jnp.mean(o * o, axis=-1, keepdims=True) + EPS) * ng_ref[...]
            gate = pb_ref[rows, 2 * qw + vw + h * B_VAL:2 * qw + vw + (h + 1) * B_VAL]
            yb_ref[rows, sl] = o * (gate * jax.nn.sigmoid(gate))
        return carry

    lax.fori_loop(0, n_chunks, bwd_body, 0)

    for d in range(2):
        for h in range(B_HEADS):
            sfin_ref[0, 0, d, h] = s_scr[d, h, pl.ds((h % 2) * B_KEY, B_KEY), :]


def _gla(pb, row0, n_seq, t_len, wgk, bgk, ng, s0, has_init, yb_prev=None):
    n_rows, cols = pb.shape
    blk0 = row0 // t_len
    st_shape = (n_seq, 1, 2, B_HEADS, B_KEY, B_VAL)
    st_spec = pl.BlockSpec((1, 1, 2, B_HEADS, B_KEY, B_VAL), lambda i: (i, 0, 0, 0, 0, 0))
    full = lambda w: pl.BlockSpec(w.shape, lambda i: (0,) * w.ndim)
    kern = functools.partial(_gla_kernel, has_init=has_init)
    in_specs = [pl.BlockSpec((t_len, cols), lambda i: (i + blk0, 0)), full(wgk), full(bgk), full(ng), st_spec]
    args = [pb, wgk, bgk, ng, s0]
    aliases = {}
    if yb_prev is not None:
        kern = _skip_refs(kern, len(args), 1)
        aliases = {len(args): 0}
        in_specs.append(pl.BlockSpec(memory_space=pl.ANY))
        args.append(yb_prev)
    return pl.pallas_call(
        kern,
        grid=(n_seq,),
        in_specs=in_specs,
        out_specs=[pl.BlockSpec((t_len, B_HEADS * B_VAL), lambda i: (i + blk0, 0)), st_spec],
        out_shape=[jax.ShapeDtypeStruct((n_rows, B_HEADS * B_VAL), F32),
                   jax.ShapeDtypeStruct(st_shape, F32)],
        input_output_aliases=aliases,
        scratch_shapes=[pltpu.VMEM((2, B_HEADS, LANES, B_VAL), F32),
                        pltpu.VMEM((t_len, B_HEADS * B_VAL), F32)],
        compiler_params=_cparams(("parallel",)),
    )(*args)


def _mix_kernel(of_ref, ob_ref, bonus_ref, g_ref, yb_ref, pg_ref, x_ref, g1_ref, sh2_ref, sc2_ref,
                lng_ref, lnb_ref, seg_ref, woa_ref, wob_ref, wo_ref, n2g_ref, wr_ref, br_ref,
                x1_ref, h2_ref, route_ref):
    d_model = x_ref.shape[1]
    seg = seg_ref[...] * (1.0 / A_DIM)
    o = of_ref[...] + ob_ref[...]
    mu = _dot(o, seg, HI)
    dev = o - mu
    var = _dot(dev * dev, seg, HI)
    on = dev * lax.rsqrt(var + GN_EPS) * lng_ref[...] + lnb_ref[...]
    ya = ((on + bonus_ref[...]) * g_ref[...]).astype(BF16)
    ta = _dot(ya, woa_ref[...])
    tb = _dot(yb_ref[...].astype(BF16), wob_ref[...])
    pg = pg_ref[...]
    u = jax.nn.sigmoid(pg[:, :d_model]) * ta + jax.nn.sigmoid(pg[:, d_model:]) * tb
    mix = _dot(u.astype(BF16), wo_ref[...])
    x1 = x_ref[...] + g1_ref[0] * mix
    x1_ref[...] = x1
    y = x1 * lax.rsqrt(jnp.mean(x1 * x1, axis=-1, keepdims=True) + EPS)
    h2 = (y * n2g_ref[...]) * (1.0 + sc2_ref[0]) + sh2_ref[0]
    for s in range(d_model // LANES):
        h2_ref[pl.ds(s, h2.shape[0], stride=SUBLANES), :] = h2[:, s * LANES:(s + 1) * LANES]

    logits = _dot(h2, wr_ref[...], HI) + br_ref[...]
    lane = lax.broadcasted_iota(jnp.int32, logits.shape, 1)
    neg = jnp.float32(-jnp.inf)
    big = jnp.int32(1 << 20)
    is_g = lane < N_GROUPS
    gl = jnp.where(is_g, logits, neg)
    gmax = jnp.max(gl, axis=-1, keepdims=True)
    grp = jnp.min(jnp.where(gl == gmax, lane, big), axis=-1, keepdims=True)
    g_w = 1.0 / jnp.sum(jnp.where(is_g, jnp.exp(gl - gmax), 0.0), axis=-1, keepdims=True)
    lo = N_GROUPS + grp * N_PER_GROUP
    in_grp = (lane >= lo) & (lane < lo + N_PER_GROUP)
    el = jnp.where(in_grp, logits, neg)
    m1 = jnp.max(el, axis=-1, keepdims=True)
    i1 = jnp.min(jnp.where(el == m1, lane, big), axis=-1, keepdims=True)
    el2 = jnp.where(lane == i1, neg, el)
    m2 = jnp.max(el2, axis=-1, keepdims=True)
    i2 = jnp.min(jnp.where(el2 == m2, lane, big), axis=-1, keepdims=True)
    e2 = jnp.exp(m2 - m1)
    w1 = g_w / (1.0 + e2)
    w2 = g_w * e2 / (1.0 + e2)
    e1 = (i1 - N_GROUPS).astype(F32)
    e2_id = (i2 - N_GROUPS).astype(F32)
    route_ref[...] = jnp.where(lane == 0, e1, jnp.where(lane == 1, e2_id, jnp.where(lane == 2, w1, w2)))


def _mix(of, ob, bonus, g, yb, pg, x, mod3, consts, n_ctx, t_lat):
    n, d = x.shape
    tm = 256
    row = functools.partial(_mod_row, tm=tm, n_ctx=n_ctx, t_lat=t_lat)
    full = lambda w: pl.BlockSpec(w.shape, lambda i: (0,) * w.ndim)
    tile = lambda a: pl.BlockSpec((tm, a.shape[1]), lambda i: (i, 0))
    mod_spec = lambda c: pl.BlockSpec((1, 1, d), lambda i: (row(i), 0, c))
    return pl.pallas_call(
        _mix_kernel,
        grid=(n // tm,),
        in_specs=[tile(of), tile(ob), tile(bonus), tile(g), tile(yb), tile(pg), tile(x),
                  mod_spec(2), mod_spec(3), mod_spec(4)] + [full(w) for w in consts],
        out_specs=[pl.BlockSpec((tm, d), lambda i: (i, 0)),
                   pl.BlockSpec((tm * d // LANES, LANES), lambda i: (i, 0)),
                   pl.BlockSpec((tm, LANES), lambda i: (i, 0))],
        out_shape=[jax.ShapeDtypeStruct((n, d), F32), jax.ShapeDtypeStruct((n * d // LANES, LANES), F32),
                   jax.ShapeDtypeStruct((n, LANES), F32)],
        compiler_params=_cparams(("parallel",)),
    )(of, ob, bonus, g, yb, pg, x, mod3, mod3, mod3, *consts)


MOE_TM = 256
MOE_TMF = 256


def _route_tables(route, n_tok):
    e = route[:, 0:2].astype(jnp.int32).reshape(-1)
    w = route[:, 2:4].reshape(-1)
    onehot = (e[:, None] == jnp.arange(N_EXPERTS, dtype=jnp.int32)[None]).astype(jnp.int32)
    csum = jnp.cumsum(onehot, axis=0)
    rank = jnp.sum(csum * onehot, axis=1) - 1
    tiles = (csum[-1] + MOE_TM - 1) // MOE_TM
    tile_end = jnp.cumsum(tiles)
    tile_start = tile_end - tiles
    pos = jnp.sum(onehot * tile_start[None, :], axis=1) * MOE_TM + rank
    t_max = 2 * n_tok // MOE_TM + N_EXPERTS
    tile_e = jnp.sum((tile_end[None, :] <= jnp.arange(t_max, dtype=jnp.int32)[:, None]).astype(jnp.int32), axis=1)
    tile_e = jnp.minimum(tile_e, N_EXPERTS - 1)
    return pos, w, tile_e, tile_end[-1:], t_max


def _expert_kernel(tile_e_ref, n_tiles_ref, pos_ref, w_ref, h2t_ref, wg_ref, wu_ref, wd_ref,
                   y_ref, row_off_ref, row_w_ref, buf_ref, *, n_tok, tm):
    t = pl.program_id(0)
    n_sub = buf_ref.shape[0] // tm
    unroll = 8

    @pl.when(t == 0)
    def _():
        def init(i, carry):
            row_off_ref[i] = n_tok * SUBLANES
            row_w_ref[i] = 0.0
            return carry

        lax.fori_loop(0, row_off_ref.shape[0], init, 0, unroll=unroll)

        def scatter(j, carry):
            p = pos_ref[j]
            row_off_ref[p] = (j // 2) * SUBLANES
            row_w_ref[p] = w_ref[j]
            return carry

        lax.fori_loop(0, pos_ref.shape[0], scatter, 0, unroll=unroll)

    @pl.when(t < n_tiles_ref[0])
    def _():
        def gather(i, carry):
            off = jnp.minimum(row_off_ref[t * tm + i], (n_tok - 1) * SUBLANES)
            src = pl.ds(pl.multiple_of(off, SUBLANES), SUBLANES)
            buf_ref[pl.ds(pl.multiple_of(i * SUBLANES, SUBLANES), SUBLANES), :] = h2t_ref[src, :]
            return carry

        lax.fori_loop(0, tm, gather, 0, unroll=unroll)
        h = jnp.concatenate([buf_ref[pl.ds(s, tm, stride=SUBLANES), :] for s in range(n_sub)], axis=1).astype(BF16)
        hg = _dot(h, wg_ref[0])
        hu = _dot(h, wu_ref[0])
        act = (hg * jax.nn.sigmoid(hg)) * hu
        y = _dot(act.astype(BF16), wd_ref[0])
        for s in range(n_sub):
            y_ref[pl.ds(s, tm, stride=SUBLANES), :] = y[:, s * LANES:(s + 1) * LANES]

    @pl.when(t >= n_tiles_ref[0])
    def _():
        y_ref[...] = jnp.zeros_like(y_ref)


def _experts(h2t, pos, w_flat, tile_e, n_tiles, t_max, wg, wu, wd, n_tok):
    tm = MOE_TM
    n_exp, d, hid = wg.shape
    n_sub = d // LANES
    rows = t_max * tm
    grid_spec = pltpu.PrefetchScalarGridSpec(
        num_scalar_prefetch=4,
        grid=(t_max,),
        in_specs=[pl.BlockSpec(h2t.shape, lambda t, te, *_: (0, 0), pipeline_mode=pl.Buffered(1)),
                  pl.BlockSpec((1, d, hid), lambda t, te, *_: (te[t], 0, 0)),
                  pl.BlockSpec((1, d, hid), lambda t, te, *_: (te[t], 0, 0)),
                  pl.BlockSpec((1, hid, d), lambda t, te, *_: (te[t], 0, 0))],
        out_specs=[pl.BlockSpec((tm * n_sub, LANES), lambda t, te, *_: (t, 0)),
                   pl.BlockSpec(memory_space=pltpu.SMEM), pl.BlockSpec(memory_space=pltpu.SMEM)],
        scratch_shapes=[pltpu.VMEM((tm * n_sub, LANES), F32)],
    )
    return pl.pallas_call(
        functools.partial(_expert_kernel, n_tok=n_tok, tm=tm),
        grid_spec=grid_spec,
        out_shape=[jax.ShapeDtypeStruct((rows * n_sub, LANES), F32),
                   jax.ShapeDtypeStruct((rows,), jnp.int32), jax.ShapeDtypeStruct((rows,), F32)],
        compiler_params=_cparams(("arbitrary",)),
    )(tile_e, n_tiles, pos, w_flat, h2t, wg, wu, wd)


def _combine_kernel(row_off_ref, row_w_ref, ys_ref, x1_ref, g2_ref, fng_ref, yc_ref, yl_ref, acc_ref, stage_ref,
                    *, t_max, ctx_tiles):
    step = pl.program_id(0)
    tm = row_off_ref.shape[2]
    tmf = x1_ref.shape[0]
    n_sub = x1_ref.shape[1] // LANES
    group = SUBLANES

    @pl.when(step == 0)
    def _():
        acc_ref[...] = jnp.zeros_like(acc_ref)

    @pl.when(step < t_max)
    def _():
        def body(ib, carry):
            slots, vals = [], []
            for k in range(group):
                i = ib * group + k
                slot = pl.ds(pl.multiple_of(row_off_ref[0, 0, i], n_sub), n_sub)
                slots.append(slot)
                vals.append(acc_ref[slot, :]
                            + row_w_ref[0, 0, i] * ys_ref[pl.ds(pl.multiple_of(i * n_sub, n_sub), n_sub), :])
            for slot, val in zip(slots, vals):
                acc_ref[slot, :] = val
            return carry

        lax.fori_loop(0, tm // group, body, 0)

    @pl.when(step >= t_max)
    def _():
        j = step - t_max
        stage_ref[...] = acc_ref[pl.ds(pl.multiple_of(j * tmf * n_sub, tmf * n_sub), tmf * n_sub), :]
        moe = jnp.concatenate([stage_ref[pl.ds(s, tmf, stride=n_sub), :] for s in range(n_sub)], axis=1)
        x2 = x1_ref[...] + g2_ref[0] * moe
        y = x2 * lax.rsqrt(jnp.mean(x2 * x2, axis=-1, keepdims=True) + EPS) * fng_ref[...]

        @pl.when(j < ctx_tiles)
        def _():
            yc_ref[...] = y

        @pl.when(j >= ctx_tiles)
        def _():
            yl_ref[...] = y


def _combine(ys, row_off, row_w, x1, mod3, fng, n_ctx, t_lat):
    n, d = x1.shape
    tm = MOE_TM
    tmf = MOE_TMF
    t_max = row_off.shape[0] // tm
    n_sub = d // LANES
    assert n_sub == SUBLANES, "one token must fill exactly one (8, 128) tile"
    ctx_tiles = n_ctx // tmf
    row = functools.partial(_mod_row, tm=tmf, n_ctx=n_ctx, t_lat=t_lat)
    src = lambda s: jnp.minimum(s, t_max - 1)
    dst = lambda s: jnp.maximum(s - t_max, 0)
    smem_tile = pl.BlockSpec((1, 1, tm), lambda s: (src(s), 0, 0), memory_space=pltpu.SMEM)
    return pl.pallas_call(
        functools.partial(_combine_kernel, t_max=t_max, ctx_tiles=ctx_tiles),
        grid=(t_max + n // tmf,),
        in_specs=[smem_tile, smem_tile,
                  pl.BlockSpec((tm * n_sub, LANES), lambda s: (src(s), 0)),
                  pl.BlockSpec((tmf, d), lambda s: (dst(s), 0)),
                  pl.BlockSpec((1, 1, d), lambda s: (row(dst(s)), 0, 5)),
                  pl.BlockSpec((1, d), lambda s: (0, 0))],
        out_specs=[pl.BlockSpec((tmf, d), lambda s: (jnp.minimum(dst(s), ctx_tiles - 1), 0)),
                   pl.BlockSpec((tmf, d), lambda s: (jnp.maximum(dst(s) - ctx_tiles, 0), 0))],
        out_shape=[jax.ShapeDtypeStruct((n_ctx, d), F32), jax.ShapeDtypeStruct((n - n_ctx, d), F32)],
        scratch_shapes=[pltpu.VMEM(((n + 1) * n_sub, LANES), F32),
                        pltpu.VMEM((tmf * n_sub, LANES), F32)],
        compiler_params=_cparams(("arbitrary",)),
    )(row_off.reshape(t_max, 1, tm), row_w.reshape(t_max, 1, tm), ys, x1, mod3, fng)


def _block_diag2(a, b):
    za = jnp.zeros((a.shape[0], b.shape[1]), a.dtype)
    zb = jnp.zeros((b.shape[0], a.shape[1]), a.dtype)
    return jnp.concatenate([jnp.concatenate([a, za], 1), jnp.concatenate([zb, b], 1)], 0)


def kernel(x_prompt, x_sample, state_rwkv, state_gla, c, c_ctx, w_mod, b_mod, norm1_g, norm2_g, w_in, shift_mu, a_w0, a_w_up, a_a0, a_a_up, a_g_up, a_k_k, a_k_a, a_r_k, a_ln_g, a_ln_b, b_gk_up, b_gk_bias, b_norm_g, w_out_a, w_out_b, w_o, w_router_group, b_router_group, w_router_expert, b_router_expert, w_exp_gate, w_exp_up, w_exp_down, final_norm_g):
    n_ctx_seq, t_ctx, d = x_prompt.shape
    n_lat_seq, t_lat, _ = x_sample.shape
    assert w_mod.shape[0] == 1, "single-layer configuration"
    n_ctx = n_ctx_seq * t_ctx
    n_lat = n_lat_seq * t_lat
    x = jnp.concatenate([x_prompt.reshape(n_ctx, d), x_sample.reshape(n_lat, d)], 0)

    cvec = jnp.concatenate([c_ctx[None], c, jnp.zeros((SUBLANES - 1 - n_lat_seq, d), F32)], 0)
    mod = _modulation(cvec, w_mod[0], b_mod[0][None])
    mod3 = mod[:, None, :]

    a_cols = shift_mu.shape[1]
    b_cols = 2 * B_HEADS * B_KEY + 2 * B_HEADS * B_VAL + 2 * b_gk_up.shape[2]
    b_pad = -b_cols % LANES
    w_in0 = w_in[0].astype(BF16)
    wa = w_in0[:, :a_cols]
    wb = jnp.pad(w_in0[:, a_cols:a_cols + b_cols], ((0, 0), (0, b_pad)))
    wg = w_in0[:, a_cols + b_cols:]
    pa, pb, pg = _in_proj(x, mod3, norm1_g, wa, wb, wg, n_ctx, t_lat)

    seg = jnp.kron(jnp.eye(A_HEADS, dtype=F32), jnp.ones((A_DIM, A_DIM), F32))
    consts_a = (shift_mu, _block_diag2(a_w_up[0, 0], a_w_up[0, 1]), _block_diag2(a_a_up[0, 0], a_a_up[0, 1]),
                a_g_up[0], a_w0[0], a_a0[0], a_k_k, a_k_a, a_r_k[0].reshape(1, A_WIDTH), seg)
    prep_c = _prep_a(pa, 0, n_ctx_seq, t_ctx, False, consts_a)
    ra, wk, bv, bonus, g_a = _prep_a(pa, n_ctx, n_lat_seq, t_lat, True, consts_a, prev_outs=prep_c)

    n_tok = n_ctx + n_lat
    assert (n_ctx // t_lat) % n_lat_seq == 0
    view3 = lambda a, t: a.reshape(n_tok // t, t * A_HEADS, LANES)
    view4 = lambda a, t: a.reshape(2, n_tok // t, t * A_HEADS, LANES)
    o_ctx, sfin = _scan_ctx(view3(ra, t_ctx), view4(wk, t_ctx), view4(bv, t_ctx), n_ctx_seq, t_ctx, 16)
    o_ctx = o_ctx.reshape(2, t_ctx, A_DIM, n_ctx_seq, A_HEADS).transpose(0, 3, 1, 4, 2).reshape(2, n_ctx, A_WIDTH)
    new_state_rwkv = sfin.reshape(2, A_DIM, A_DIM, n_ctx_seq, A_HEADS).transpose(3, 0, 4, 2, 1)[:, None]

    vl = A_DIM // 2
    s0 = state_rwkv[:, 0].reshape(n_lat_seq, 2, A_HEADS, 2, vl, A_DIM)
    s0 = s0.transpose(5, 4, 3, 1, 0, 2).reshape(A_DIM, vl, LANES)
    of_l, ob_l = _scan_lat(view3(ra, t_lat), view4(wk, t_lat), view4(bv, t_lat), s0, n_lat_seq, t_lat, 16,
                           n_ctx // t_lat // n_lat_seq)

    def lat_rows(o, dirn):
        o = o.reshape(t_lat, vl, 2, 2, n_lat_seq, A_HEADS)[:, :, :, dirn]
        return o.transpose(3, 0, 4, 2, 1).reshape(n_lat, A_WIDTH)

    o_f = jnp.concatenate([o_ctx[0], lat_rows(of_l, 0)], 0)
    o_b = jnp.concatenate([o_ctx[1], lat_rows(ob_l, 1)], 0)

    rank = b_gk_up.shape[2]
    wgk = jnp.zeros((2, LANES, B_HEADS * B_KEY), F32)
    wgk = wgk.at[0, :rank].set(b_gk_up[0, 0]).at[1, rank:2 * rank].set(b_gk_up[0, 1])
    zero_gla = jnp.zeros((n_ctx_seq, 1, 2, B_HEADS, B_KEY, B_VAL), F32)
    yb_c, new_state_gla = _gla(pb, 0, n_ctx_seq, t_ctx, wgk, b_gk_bias[0], b_norm_g, zero_gla, False)
    yb, _ = _gla(pb, n_ctx, n_lat_seq, t_lat, wgk, b_gk_bias[0], b_norm_g, state_gla, True, yb_prev=yb_c)

    w_r = jnp.concatenate([w_router_group[0], w_router_expert[0].transpose(1, 0, 2).reshape(d, N_EXPERTS)], 1)
    w_r = jnp.pad(w_r, ((0, 0), (0, LANES - w_r.shape[1])))
    b_r = jnp.concatenate([b_router_group[0], b_router_expert[0].reshape(N_EXPERTS)])
    b_r = jnp.pad(b_r, (0, LANES - b_r.shape[0]))[None]
    consts_m = (a_ln_g, a_ln_b, seg, w_out_a[0].astype(BF16), w_out_b[0].astype(BF16), w_o[0].astype(BF16),
                norm2_g, w_r, b_r)
    x1, h2t, route = _mix(o_f, o_b, bonus, g_a, yb, pg, x, mod3, consts_m, n_ctx, t_lat)

    pos, w_flat, tile_e, n_tiles, t_max = _route_tables(route, n_tok)
    ys, row_off, row_w = _experts(h2t, pos, w_flat, tile_e, n_tiles, t_max, w_exp_gate[0].astype(BF16),
                                  w_exp_up[0].astype(BF16), w_exp_down[0].astype(BF16), n_tok)
    y_c, y_l = _combine(ys, row_off, row_w, x1, mod3, final_norm_g[None], n_ctx, t_lat)
    return (y_c.reshape(n_ctx_seq, t_ctx, d), y_l.reshape(n_lat_seq, t_lat, d), new_state_rwkv, new_state_gla)
```

```python
import functools

import jax
import jax.numpy as jnp
from jax import lax
from jax.experimental import pallas as pl
from jax.experimental.pallas import tpu as pltpu

F32 = jnp.float32
BF16 = jnp.bfloat16
HI = lax.Precision.HIGHEST

LANES = 128
SUBLANES = 8
VMEM_LIMIT = 56 * 1024 * 1024

EPS = 1e-6
GN_EPS = 64e-5
GRID_W = 64
A_HEADS, A_DIM = 8, 64
A_WIDTH = A_HEADS * A_DIM
B_HEADS, B_KEY, B_VAL = 4, 64, 128
GLA_CHUNK = 64
GLA_GATE_NORM = 16.0
N_GROUPS, N_PER_GROUP = 4, 8
N_EXPERTS = N_GROUPS * N_PER_GROUP


def _cparams(sem, flags=None):
    return pltpu.CompilerParams(dimension_semantics=sem, vmem_limit_bytes=VMEM_LIMIT, flags=flags)


def _log_sigmoid(z):
    return jnp.minimum(z, 0.0) - jnp.log(1.0 + jnp.exp(-jnp.abs(z)))


def _dot(a, b, precision=None):
    return jnp.dot(a, b, preferred_element_type=F32, precision=precision)


def _mod_kernel(c_ref, w_ref, b_ref, o_ref):
    c = c_ref[...]
    o_ref[...] = _dot(c * jax.nn.sigmoid(c), w_ref[...], HI) + b_ref[...]


def _modulation(cvec, w_mod, b_mod):
    rows, d = cvec.shape
    n = w_mod.shape[1]
    tn = n // 4
    return pl.pallas_call(
        _mod_kernel,
        grid=(n // tn,),
        in_specs=[pl.BlockSpec((rows, d), lambda j: (0, 0)),
                  pl.BlockSpec((d, tn), lambda j: (0, j)),
                  pl.BlockSpec((1, tn), lambda j: (0, j))],
        out_specs=pl.BlockSpec((rows, tn), lambda j: (0, j)),
        out_shape=jax.ShapeDtypeStruct((rows, n), F32),
        compiler_params=_cparams(("arbitrary",)),
    )(cvec, w_mod, b_mod)


def _mod_row(i, tm, n_ctx, t_lat):
    start = i * tm
    return jnp.where(start < n_ctx, 0, 1 + (start - n_ctx) // t_lat)


def _in_proj_kernel(x_ref, sh_ref, sc_ref, g_ref, wa_ref, wb_ref, wg_ref, pa_ref, pb_ref, pg_ref):
    x = x_ref[...]
    y = x * lax.rsqrt(jnp.mean(x * x, axis=-1, keepdims=True) + EPS)
    h = ((y * g_ref[...]) * (1.0 + sc_ref[0]) + sh_ref[0]).astype(BF16)
    pa_ref[...] = _dot(h, wa_ref[...])
    pb_ref[...] = _dot(h, wb_ref[...])
    pg_ref[...] = _dot(h, wg_ref[...])


def _in_proj(x, mod3, norm_g, wa, wb, wg, n_ctx, t_lat):
    n, d = x.shape
    tm = 256
    row = functools.partial(_mod_row, tm=tm, n_ctx=n_ctx, t_lat=t_lat)
    full = lambda w: pl.BlockSpec(w.shape, lambda i: (0, 0))
    outs = [jax.ShapeDtypeStruct((n, w.shape[1]), F32) for w in (wa, wb, wg)]
    return pl.pallas_call(
        _in_proj_kernel,
        grid=(n // tm,),
        in_specs=[pl.BlockSpec((tm, d), lambda i: (i, 0)),
                  pl.BlockSpec((1, 1, d), lambda i: (row(i), 0, 0)),
                  pl.BlockSpec((1, 1, d), lambda i: (row(i), 0, 1)),
                  pl.BlockSpec((1, d), lambda i: (0, 0)),
                  full(wa), full(wb), full(wg)],
        out_specs=[pl.BlockSpec((tm, w.shape[1]), lambda i: (i, 0)) for w in (wa, wb, wg)],
        out_shape=outs,
        compiler_params=_cparams(("parallel",)),
    )(x, mod3, mod3, norm_g, wa, wb, wg)


def _pair_heads(x, y):
    lane = lax.broadcasted_iota(jnp.int32, (1, LANES), 1)
    low = lane < A_DIM
    tiles = []
    for p in range(A_HEADS // 2):
        xv = x[:, p * LANES:(p + 1) * LANES]
        yv = y[:, p * LANES:(p + 1) * LANES]
        tiles.append(jnp.where(low, xv, pltpu.roll(yv, A_DIM, axis=1)))
        tiles.append(jnp.where(low, pltpu.roll(xv, A_DIM, axis=1), yv))
    return tiles


def _prep_a_kernel(pa_ref, prev_ref, next_ref, mu_ref, wup_ref, aup_ref, gup_ref, w0_ref, a0_ref, kk_ref, ka_ref,
                   rk_ref, seg_ref, ra_ref, wk_ref, bv_ref, bonus_ref, g_ref, *, on_grid, seq_len):
    pa = pa_ref[...]
    t_len, cols = pa.shape
    halo = prev_ref.shape[0]
    ext = jnp.concatenate([prev_ref[...], pa, next_ref[...]], axis=0)
    col = lax.broadcasted_iota(jnp.int32, (1, cols), 1)
    row = lax.broadcasted_iota(jnp.int32, (t_len, 1), 0) + pl.program_id(1) * t_len
    quarter = col // (cols // 4)

    def shifted(s):
        if s % SUBLANES == 0:
            moved = ext[halo - s:halo - s + t_len]
        else:
            moved = pltpu.roll(ext, s % ext.shape[0], axis=0)[halo:halo + t_len]
        valid = (row >= s) & (row < seq_len + s)
        return jnp.where(valid, moved, 0.0)

    if on_grid:
        gcol = row % GRID_W
        left = jnp.where(gcol != 0, shifted(1), 0.0)
        right = jnp.where(gcol != GRID_W - 1, shifted(-1), 0.0)
        nb = jnp.where(quarter == 0, left,
                       jnp.where(quarter == 1, right,
                                 jnp.where(quarter == 2, shifted(GRID_W), shifted(-GRID_W))))
    else:
        nb = jnp.where(quarter % 2 == 0, shifted(1), shifted(-1))
    pa = pa + (nb - pa) * mu_ref[...]

    r = pa[:, 0:A_WIDTH]
    k = pa[:, A_WIDTH:2 * A_WIDTH]
    v = pa[:, 2 * A_WIDTH:3 * A_WIDTH]
    c0 = 3 * A_WIDTH
    wd = pa[:, c0:c0 + LANES]
    ad = pa[:, c0 + LANES:c0 + 2 * LANES]
    gd = pa[:, c0 + 2 * LANES:c0 + 3 * LANES]
    seg = seg_ref[...]

    up_w = _dot(jnp.tanh(wd), wup_ref[...], HI)
    up_a = _dot(ad, aup_ref[...], HI)
    g_ref[...] = _dot(jax.nn.sigmoid(gd), gup_ref[...], HI)

    kk = k * kk_ref[...]
    kk = kk / jnp.maximum(jnp.sqrt(_dot(kk * kk, seg, HI)), 1e-12)
    neg_kk = -kk
    kd_sum = None
    for d in range(2):
        sl = slice(d * A_WIDTH, (d + 1) * A_WIDTH)
        w_log = _log_sigmoid(w0_ref[d:d + 1, :] + up_w[:, sl]) - 0.5
        decay = jnp.exp(-jnp.exp(w_log))
        a = jax.nn.sigmoid(a0_ref[d:d + 1, :] + up_a[:, sl])
        kd = k * (1.0 + (a - 1.0) * ka_ref[...])
        kd_sum = kd if kd_sum is None else kd_sum + kd
        for h, tile in enumerate(_pair_heads(decay, kd)):
            wk_ref[d, pl.ds(h, t_len, stride=A_HEADS), :] = tile
        for h, tile in enumerate(_pair_heads(kk * a, v)):
            bv_ref[d, pl.ds(h, t_len, stride=A_HEADS), :] = tile
    for h, tile in enumerate(_pair_heads(r, neg_kk)):
        ra_ref[pl.ds(h, t_len, stride=A_HEADS), :] = tile
    bonus_ref[...] = _dot(r * kd_sum * rk_ref[...], seg, HI) * v


def _skip_refs(fn, start, count):
    return lambda *refs, **kw: fn(*refs[:start], *refs[start + count:], **kw)


def _prep_a(pa, row0, n_seq, t_len, on_grid, consts, prev_outs=None):
    n_rows, cols = pa.shape
    tt = 256
    halo = GRID_W if on_grid else SUBLANES
    tiles = t_len // tt
    blk0 = row0 // tt
    main = lambda i, j: (blk0 + i * tiles + j, 0)
    prev = lambda i, j: (jnp.maximum((blk0 + i * tiles + j) * (tt // halo) - 1, 0), 0)
    nxt = lambda i, j: (jnp.minimum((blk0 + i * tiles + j + 1) * (tt // halo), n_rows // halo - 1), 0)
    full = lambda w: pl.BlockSpec(w.shape, lambda i, j: (0,) * w.ndim)
    hr = jax.ShapeDtypeStruct((n_rows * A_HEADS, LANES), F32)
    hr2 = jax.ShapeDtypeStruct((2, n_rows * A_HEADS, LANES), F32)
    tok = jax.ShapeDtypeStruct((n_rows, A_WIDTH), F32)
    hr_spec = pl.BlockSpec((tt * A_HEADS, LANES), main)
    hr2_spec = pl.BlockSpec((2, tt * A_HEADS, LANES), lambda i, j: (0,) + main(i, j))
    tok_spec = pl.BlockSpec((tt, A_WIDTH), main)
    kern = functools.partial(_prep_a_kernel, on_grid=on_grid, seq_len=t_len)
    in_specs = [pl.BlockSpec((tt, cols), main), pl.BlockSpec((halo, cols), prev),
                pl.BlockSpec((halo, cols), nxt)] + [full(w) for w in consts]
    args = [pa, pa, pa, *consts]
    aliases = {}
    if prev_outs is not None:
        n_in = len(args)
        kern = _skip_refs(kern, n_in, len(prev_outs))
        aliases = {n_in + k: k for k in range(len(prev_outs))}
        in_specs += [pl.BlockSpec(memory_space=pl.ANY)] * len(prev_outs)
        args += list(prev_outs)
    return pl.pallas_call(
        kern,
        grid=(n_seq, tiles),
        in_specs=in_specs,
        out_specs=[hr_spec, hr2_spec, hr2_spec, tok_spec, tok_spec],
        out_shape=[hr, hr2, hr2, tok, tok],
        input_output_aliases=aliases,
        compiler_params=_cparams(("parallel", "parallel")),
    )(*args)


SCAN_SETS = 4


def _scan_block(load_tile, store_o, s_ref, kra, kwk, kbv, tb, v_rows, lane_rep):
    n_sets = kra.shape[0]
    ahead = 2
    assert tb % n_sets == 0 and n_sets > ahead

    def transpose_into(slot, j):
        kra[slot] = load_tile(0, j).T
        kwk[slot] = load_tile(1, j).T
        kbv[slot] = load_tile(2, j).T

    for j in range(ahead):
        transpose_into(j, j)

    def row_b(buf, slot, r):
        return jnp.broadcast_to(buf[slot, pl.ds(r, 1), :], (SUBLANES, LANES))

    n_kb = A_DIM // SUBLANES
    n_vb = v_rows // SUBLANES
    vrows = lambda vb: pl.ds(vb * SUBLANES, SUBLANES)

    def sa_init(kb, acc):
        acc = list(acc)
        for kk in range(SUBLANES):
            k = kb * SUBLANES + kk
            ab = row_b(kra, 0, A_DIM + k)
            for vb in range(n_vb):
                acc[vb] = acc[vb] + s_ref[k, vrows(vb), :] * ab
        return tuple(acc)

    zeros = tuple(jnp.zeros((SUBLANES, LANES), F32) for _ in range(n_vb))
    sa0 = lax.fori_loop(0, n_kb, sa_init, zeros)
    lane = lax.broadcasted_iota(jnp.int32, (1, LANES), 1)

    def step(j, slot, sa):
        transpose_into((slot + ahead) % n_sets, jnp.minimum(j + ahead, tb - 1))
        nxt = (slot + 1) % n_sets
        vt = kbv[slot, pl.ds(A_DIM, A_DIM), :]
        if lane_rep == 1:
            v = vt
        else:
            v = jnp.where(lane < LANES // 2, vt[:v_rows], vt[v_rows:])
        v = [v[vb * SUBLANES:(vb + 1) * SUBLANES] for vb in range(n_vb)]
        o = list(zeros)
        sa_next = list(zeros)
        for k in range(A_DIM):
            wb, kb = row_b(kwk, slot, k), row_b(kwk, slot, A_DIM + k)
            bb, rb = row_b(kbv, slot, k), row_b(kra, slot, k)
            ab = row_b(kra, nxt, A_DIM + k)
            for vb in range(n_vb):
                s_new = s_ref[k, vrows(vb), :] * wb + kb * v[vb] + bb * sa[vb]
                s_ref[k, vrows(vb), :] = s_new
                o[vb] = o[vb] + s_new * rb
                sa_next[vb] = sa_next[vb] + s_new * ab
        store_o(j, jnp.concatenate(o, axis=0))
        return tuple(sa_next)

    def rotation(jj, sa):
        for slot in range(n_sets):
            sa = step(jj * n_sets + slot, slot, sa)
        return sa

    lax.fori_loop(0, tb // n_sets, rotation, sa0)


def _scan_ctx_kernel(ra_ref, wk_ref, bv_ref, o_ref, sfin_ref, s_ref, kra, kwk, kbv, *, tb, n_seq):
    g = pl.program_id(0)
    i = pl.program_id(1)

    @pl.when(i == 0)
    def _():
        s_ref[...] = jnp.zeros_like(s_ref)

    def local_t(j):
        return j + g * (tb - 1 - 2 * j)

    refs = (ra_ref, wk_ref, bv_ref)

    def load_tile(which, j):
        ref = refs[which]
        rows = pl.ds(pl.multiple_of(local_t(j) * A_HEADS, A_HEADS), A_HEADS)
        slabs = [ref[b, rows, :] if which == 0 else ref[0, b, rows, :] for b in range(n_seq)]
        return jnp.concatenate(slabs, axis=0)

    def store_o(j, o):
        o_ref[0, local_t(j)] = o

    _scan_block(load_tile, store_o, s_ref, kra, kwk, kbv, tb, A_DIM, 1)

    @pl.when(i == pl.num_programs(1) - 1)
    def _():
        sfin_ref[0] = s_ref[...]


def _scan_ctx(ra, wk, bv, n_seq, t_len, tb):
    nb = t_len // tb
    blk = lambda g, i: jnp.where(g == 0, i, nb - 1 - i)
    rows = tb * A_HEADS
    return pl.pallas_call(
        functools.partial(_scan_ctx_kernel, tb=tb, n_seq=n_seq),
        grid=(2, nb),
        in_specs=[pl.BlockSpec((n_seq, rows, LANES), lambda g, i: (0, blk(g, i), 0)),
                  pl.BlockSpec((1, n_seq, rows, LANES), lambda g, i: (g, 0, blk(g, i), 0)),
                  pl.BlockSpec((1, n_seq, rows, LANES), lambda g, i: (g, 0, blk(g, i), 0))],
        out_specs=[pl.BlockSpec((1, tb, A_DIM, LANES), lambda g, i: (g, blk(g, i), 0, 0)),
                   pl.BlockSpec((1, A_DIM, A_DIM, LANES), lambda g, i: (g, 0, 0, 0))],
        out_shape=[jax.ShapeDtypeStruct((2, t_len, A_DIM, LANES), F32),
                   jax.ShapeDtypeStruct((2, A_DIM, A_DIM, LANES), F32)],
        scratch_shapes=[pltpu.VMEM((A_DIM, A_DIM, LANES), F32),
                        pltpu.VMEM((SCAN_SETS, LANES, LANES), F32),
                        pltpu.VMEM((SCAN_SETS, LANES, LANES), F32),
                        pltpu.VMEM((SCAN_SETS, LANES, LANES), F32)],
        compiler_params=_cparams(("arbitrary", "arbitrary")),
    )(ra, wk, bv)


def _scan_lat_kernel(raf_ref, rab_ref, wkf_ref, wkb_ref, bvf_ref, bvb_ref, s0_ref, of_ref, ob_ref,
                     s_ref, kra, kwk, kbv, *, tb, n_seq):
    i = pl.program_id(0)

    @pl.when(i == 0)
    def _():
        s_ref[...] = s0_ref[...]

    refs = ((raf_ref, rab_ref), (wkf_ref, wkb_ref), (bvf_ref, bvb_ref))

    def load_tile(which, j):
        fwd, bwd = refs[which]
        rows_f = pl.ds(pl.multiple_of(j * A_HEADS, A_HEADS), A_HEADS)
        rows_b = pl.ds(pl.multiple_of((tb - 1 - j) * A_HEADS, A_HEADS), A_HEADS)
        if which == 0:
            half = ([fwd[b, rows_f, :] for b in range(n_seq)] + [bwd[b, rows_b, :] for b in range(n_seq)])
        else:
            half = ([fwd[0, b, rows_f, :] for b in range(n_seq)] + [bwd[0, b, rows_b, :] for b in range(n_seq)])
        return jnp.concatenate(half + half, axis=0)

    def store_o(j, o):
        of_ref[j] = o
        ob_ref[tb - 1 - j] = o

    _scan_block(load_tile, store_o, s_ref, kra, kwk, kbv, tb, A_DIM // 2, 2)


def _scan_lat(ra, wk, bv, s0, n_seq, t_len, tb, ub):
    nb = t_len // tb
    rows = tb * A_HEADS
    vl = A_DIM // 2
    f3 = pl.BlockSpec((n_seq, rows, LANES), lambda i: (ub, i, 0))
    b3 = pl.BlockSpec((n_seq, rows, LANES), lambda i: (ub, nb - 1 - i, 0))
    f4 = lambda d: pl.BlockSpec((1, n_seq, rows, LANES), lambda i: (d, ub, i, 0))
    b4 = lambda d: pl.BlockSpec((1, n_seq, rows, LANES), lambda i: (d, ub, nb - 1 - i, 0))
    o_shape = jax.ShapeDtypeStruct((t_len, vl, LANES), F32)
    return pl.pallas_call(
        functools.partial(_scan_lat_kernel, tb=tb, n_seq=n_seq),
        grid=(nb,),
        in_specs=[f3, b3, f4(0), b4(1), f4(0), b4(1),
                  pl.BlockSpec((A_DIM, vl, LANES), lambda i: (0, 0, 0))],
        out_specs=[pl.BlockSpec((tb, vl, LANES), lambda i: (i, 0, 0)),
                   pl.BlockSpec((tb, vl, LANES), lambda i: (nb - 1 - i, 0, 0))],
        out_shape=[o_shape, o_shape],
        scratch_shapes=[pltpu.VMEM((A_DIM, vl, LANES), F32),
                        pltpu.VMEM((SCAN_SETS, LANES, LANES), F32),
                        pltpu.VMEM((SCAN_SETS, LANES, LANES), F32),
                        pltpu.VMEM((SCAN_SETS, LANES, LANES), F32)],
        compiler_params=_cparams(("arbitrary",)),
    )(ra, ra, wk, wk, bv, bv, s0)


def _gla_kernel(pb_ref, wgk_ref, bgk_ref, ng_ref, s0_ref, yb_ref, sfin_ref, s_scr, o_scr, *, has_init):
    t_len = pb_ref.shape[0]
    n_chunks = t_len // GLA_CHUNK
    c_len = GLA_CHUNK
    qw = B_HEADS * B_KEY
    vw = B_HEADS * B_VAL
    ri = lax.broadcasted_iota(jnp.int32, (c_len, c_len), 0)
    ci = lax.broadcasted_iota(jnp.int32, (c_len, c_len), 1)
    tri = (ri >= ci, ri <= ci)
    lane = lax.broadcasted_iota(jnp.int32, (1, LANES), 1)
    half_mask = (lane < B_KEY, lane >= B_KEY)
    ones_cl = jnp.ones((c_len, LANES), F32)

    s_scr[...] = jnp.zeros_like(s_scr)
    if has_init:
        for d in range(2):
            for h in range(B_HEADS):
                s_scr[d, h, pl.ds((h % 2) * B_KEY, B_KEY), :] = s0_ref[0, 0, d, h]

    def chunk(c, d):
        rows = pl.ds(pl.multiple_of(c * c_len, c_len), c_len)
        q = pb_ref[rows, 0:qw] * (B_KEY ** -0.5)
        k = pb_ref[rows, qw:2 * qw]
        gdt = pb_ref[rows, 2 * qw + 2 * vw:2 * qw + 2 * vw + LANES]
        gk = _log_sigmoid(_dot(gdt, wgk_ref[d], HI) + bgk_ref[d:d + 1, :]) / GLA_GATE_NORM
        b = _dot(tri[d].astype(F32), gk, HI)
        outs = []
        for p in range(B_HEADS // 2):
            sl = slice(p * LANES, (p + 1) * LANES)
            bp = b[:, sl]
            tot_row = bp[c_len - 1:c_len, :] if d == 0 else bp[0:1, :]
            tot_col = lax.dot_general(gk[:, sl], ones_cl, (((0,), (0,)), ((), ())),
                                      precision=HI, preferred_element_type=F32)
            qe = q[:, sl] * jnp.exp(bp)
            ke = (k[:, sl] * jnp.exp(-bp)).astype(BF16)
            kd = k[:, sl] * jnp.exp(tot_row - bp)
            decay_col = jnp.exp(tot_col)
            for hh in range(2):
                h = 2 * p + hh
                v = pb_ref[rows, 2 * qw + h * B_VAL:2 * qw + (h + 1) * B_VAL].astype(BF16)
                qm = jnp.where(half_mask[hh], qe, 0.0).astype(BF16)
                km = jnp.where(half_mask[hh], kd, 0.0).astype(BF16)
                att = lax.dot_general(qm, ke, (((1,), (1,)), ((), ())), preferred_element_type=F32)
                att = jnp.where(tri[d], att, 0.0).astype(BF16)
                s_old = s_scr[d, h]
                o = _dot(att, v) + _dot(qm, s_old.astype(BF16))
                kv = lax.dot_general(km, v, (((0,), (0,)), ((), ())), preferred_element_type=F32)
                s_scr[d, h] = decay_col * s_old + kv
                outs.append(o)
        return rows, outs

    def fwd_body(c, carry):
        rows, outs = chunk(c, 0)
        for h in range(B_HEADS):
            o_scr[rows, h * B_VAL:(h + 1) * B_VAL] = outs[h]
        return carry

    lax.fori_loop(0, n_chunks, fwd_body, 0)

    def bwd_body(cc, carry):
        c = n_chunks - 1 - cc
        rows, outs = chunk(c, 1)
        for h in range(B_HEADS):
            sl = slice(h * B_VAL, (h + 1) * B_VAL)
            o = o_scr[rows, sl] + outs[h]
            o = o * lax.rsqrt(jnp.mean(o * o, axis=-1, keepdims=True) + EPS) * ng_ref[...]
            gate = pb_ref[rows, 2 * qw + vw + h * B_VAL:2 * qw + vw + (h + 1) * B_VAL]
            yb_ref[rows, sl] = o * (gate * jax.nn.sigmoid(gate))
        return carry

    lax.fori_loop(0, n_chunks, bwd_body, 0)

    for d in range(2):
        for h in range(B_HEADS):
            sfin_ref[0, 0, d, h] = s_scr[d, h, pl.ds((h % 2) * B_KEY, B_KEY), :]


def _gla(pb, row0, n_seq, t_len, wgk, bgk, ng, s0, has_init, yb_prev=None):
    n_rows, cols = pb.shape
    blk0 = row0 // t_len
    st_shape = (n_seq, 1, 2, B_HEADS, B_KEY, B_VAL)
    st_spec = pl.BlockSpec((1, 1, 2, B_HEADS, B_KEY, B_VAL), lambda i: (i, 0, 0, 0, 0, 0))
    full = lambda w: pl.BlockSpec(w.shape, lambda i: (0,) * w.ndim)
    kern = functools.partial(_gla_kernel, has_init=has_init)
    in_specs = [pl.BlockSpec((t_len, cols), lambda i: (i + blk0, 0)), full(wgk), full(bgk), full(ng), st_spec]
    args = [pb, wgk, bgk, ng, s0]
    aliases = {}
    if yb_prev is not None:
        kern = _skip_refs(kern, len(args), 1)
        aliases = {len(args): 0}
        in_specs.append(pl.BlockSpec(memory_space=pl.ANY))
        args.append(yb_prev)
    return pl.pallas_call(
        kern,
        grid=(n_seq,),
        in_specs=in_specs,
        out_specs=[pl.BlockSpec((t_len, B_HEADS * B_VAL), lambda i: (i + blk0, 0)), st_spec],
        out_shape=[jax.ShapeDtypeStruct((n_rows, B_HEADS * B_VAL), F32),
                   jax.ShapeDtypeStruct(st_shape, F32)],
        input_output_aliases=aliases,
        scratch_shapes=[pltpu.VMEM((2, B_HEADS, LANES, B_VAL), F32),
                        pltpu.VMEM((t_len, B_HEADS * B_VAL), F32)],
        compiler_params=_cparams(("parallel",)),
    )(*args)


def _mix_kernel(of_ref, ob_ref, bonus_ref, g_ref, yb_ref, pg_ref, x_ref, g1_ref, sh2_ref, sc2_ref,
                lng_ref, lnb_ref, seg_ref, woa_ref, wob_ref, wo_ref, n2g_ref, wr_ref, br_ref,
                x1_ref, h2_ref, route_ref):
    d_model = x_ref.shape[1]
    seg = seg_ref[...] * (1.0 / A_DIM)
    o = of_ref[...] + ob_ref[...]
    mu = _dot(o, seg, HI)
    dev = o - mu
    var = _dot(dev * dev, seg, HI)
    on = dev * lax.rsqrt(var + GN_EPS) * lng_ref[...] + lnb_ref[...]
    ya = ((on + bonus_ref[...]) * g_ref[...]).astype(BF16)
    ta = _dot(ya, woa_ref[...])
    tb = _dot(yb_ref[...].astype(BF16), wob_ref[...])
    pg = pg_ref[...]
    u = jax.nn.sigmoid(pg[:, :d_model]) * ta + jax.nn.sigmoid(pg[:, d_model:]) * tb
    mix = _dot(u.astype(BF16), wo_ref[...])
    x1 = x_ref[...] + g1_ref[0] * mix
    x1_ref[...] = x1
    y = x1 * lax.rsqrt(jnp.mean(x1 * x1, axis=-1, keepdims=True) + EPS)
    h2 = (y * n2g_ref[...]) * (1.0 + sc2_ref[0]) + sh2_ref[0]
    for s in range(d_model // LANES):
        h2_ref[pl.ds(s, h2.shape[0], stride=SUBLANES), :] = h2[:, s * LANES:(s + 1) * LANES]

    logits = _dot(h2, wr_ref[...], HI) + br_ref[...]
    lane = lax.broadcasted_iota(jnp.int32, logits.shape, 1)
    neg = jnp.float32(-jnp.inf)
    big = jnp.int32(1 << 20)
    is_g = lane < N_GROUPS
    gl = jnp.where(is_g, logits, neg)
    gmax = jnp.max(gl, axis=-1, keepdims=True)
    grp = jnp.min(jnp.where(gl == gmax, lane, big), axis=-1, keepdims=True)
    g_w = 1.0 / jnp.sum(jnp.where(is_g, jnp.exp(gl - gmax), 0.0), axis=-1, keepdims=True)
    lo = N_GROUPS + grp * N_PER_GROUP
    in_grp = (lane >= lo) & (lane < lo + N_PER_GROUP)
    el = jnp.where(in_grp, logits, neg)
    m1 = jnp.max(el, axis=-1, keepdims=True)
    i1 = jnp.min(jnp.where(el == m1, lane, big), axis=-1, keepdims=True)
    el2 = jnp.where(lane == i1, neg, el)
    m2 = jnp.max(el2, axis=-1, keepdims=True)
    i2 = jnp.min(jnp.where(el2 == m2, lane, big), axis=-1, keepdims=True)
    e2 = jnp.exp(m2 - m1)
    w1 = g_w / (1.0 + e2)
    w2 = g_w * e2 / (1.0 + e2)
    e1 = (i1 - N_GROUPS).astype(F32)
    e2_id = (i2 - N_GROUPS).astype(F32)
    route_ref[...] = jnp.where(lane == 0, e1, jnp.where(lane == 1, e2_id, jnp.where(lane == 2, w1, w2)))


def _mix(of, ob, bonus, g, yb, pg, x, mod3, consts, n_ctx, t_lat):
    n, d = x.shape
    tm = 256
    row = functools.partial(_mod_row, tm=tm, n_ctx=n_ctx, t_lat=t_lat)
    full = lambda w: pl.BlockSpec(w.shape, lambda i: (0,) * w.ndim)
    tile = lambda a: pl.BlockSpec((tm, a.shape[1]), lambda i: (i, 0))
    mod_spec = lambda c: pl.BlockSpec((1, 1, d), lambda i: (row(i), 0, c))
    return pl.pallas_call(
        _mix_kernel,
        grid=(n // tm,),
        in_specs=[tile(of), tile(ob), tile(bonus), tile(g), tile(yb), tile(pg), tile(x),
                  mod_spec(2), mod_spec(3), mod_spec(4)] + [full(w) for w in consts],
        out_specs=[pl.BlockSpec((tm, d), lambda i: (i, 0)),
                   pl.BlockSpec((tm * d // LANES, LANES), lambda i: (i, 0)),
                   pl.BlockSpec((tm, LANES), lambda i: (i, 0))],
        out_shape=[jax.ShapeDtypeStruct((n, d), F32), jax.ShapeDtypeStruct((n * d // LANES, LANES), F32),
                   jax.ShapeDtypeStruct((n, LANES), F32)],
        compiler_params=_cparams(("parallel",)),
    )(of, ob, bonus, g, yb, pg, x, mod3, mod3, mod3, *consts)


MOE_TM = 256
MOE_TMF = 256


def _route_tables(route, n_tok):
    e = route[:, 0:2].astype(jnp.int32).reshape(-1)
    w = route[:, 2:4].reshape(-1)
    onehot = (e[:, None] == jnp.arange(N_EXPERTS, dtype=jnp.int32)[None]).astype(jnp.int32)
    csum = jnp.cumsum(onehot, axis=0)
    rank = jnp.sum(csum * onehot, axis=1) - 1
    tiles = (csum[-1] + MOE_TM - 1) // MOE_TM
    tile_end = jnp.cumsum(tiles)
    tile_start = tile_end - tiles
    pos = jnp.sum(onehot * tile_start[None, :], axis=1) * MOE_TM + rank
    t_max = 2 * n_tok // MOE_TM + N_EXPERTS
    tile_e = jnp.sum((tile_end[None, :] <= jnp.arange(t_max, dtype=jnp.int32)[:, None]).astype(jnp.int32), axis=1)
    tile_e = jnp.minimum(tile_e, N_EXPERTS - 1)
    return pos, w, tile_e, tile_end[-1:], t_max


def _expert_kernel(tile_e_ref, n_tiles_ref, pos_ref, w_ref, h2t_ref, wg_ref, wu_ref, wd_ref,
                   y_ref, row_off_ref, row_w_ref, buf_ref, *, n_tok, tm):
    t = pl.program_id(0)
    n_sub = buf_ref.shape[0] // tm
    unroll = 8

    @pl.when(t == 0)
    def _():
        def init(i, carry):
            row_off_ref[i] = n_tok * SUBLANES
            row_w_ref[i] = 0.0
            return carry

        lax.fori_loop(0, row_off_ref.shape[0], init, 0, unroll=unroll)

        def scatter(n, carry):
            for slot in range(2):
                p = pos_ref[2 * n + slot]
                row_off_ref[p] = n * SUBLANES
                row_w_ref[p] = w_ref[2 * n + slot]
            return carry

        lax.fori_loop(0, n_tok, scatter, 0, unroll=unroll // 2)

    @pl.when(t < n_tiles_ref[0])
    def _():
        def gather(i, carry):
            off = jnp.minimum(row_off_ref[t * tm + i], (n_tok - 1) * SUBLANES)
            src = pl.ds(pl.multiple_of(off, SUBLANES), SUBLANES)
            buf_ref[pl.ds(pl.multiple_of(i * SUBLANES, SUBLANES), SUBLANES), :] = h2t_ref[src, :]
            return carry

        lax.fori_loop(0, tm, gather, 0, unroll=unroll)
        h = jnp.concatenate([buf_ref[pl.ds(s, tm, stride=SUBLANES), :] for s in range(n_sub)], axis=1).astype(BF16)
        hg = _dot(h, wg_ref[0])
        hu = _dot(h, wu_ref[0])
        act = (hg * jax.nn.sigmoid(hg)) * hu
        y = _dot(act.astype(BF16), wd_ref[0])
        for s in range(n_sub):
            y_ref[pl.ds(s, tm, stride=SUBLANES), :] = y[:, s * LANES:(s + 1) * LANES]

    @pl.when(t >= n_tiles_ref[0])
    def _():
        y_ref[...] = jnp.zeros_like(y_ref)


def _experts(h2t, pos, w_flat, tile_e, n_tiles, t_max, wg, wu, wd, n_tok):
    tm = MOE_TM
    n_exp, d, hid = wg.shape
    n_sub = d // LANES
    rows = t_max * tm
    grid_spec = pltpu.PrefetchScalarGridSpec(
        num_scalar_prefetch=4,
        grid=(t_max,),
        in_specs=[pl.BlockSpec(h2t.shape, lambda t, te, *_: (0, 0), pipeline_mode=pl.Buffered(1)),
                  pl.BlockSpec((1, d, hid), lambda t, te, *_: (te[t], 0, 0)),
                  pl.BlockSpec((1, d, hid), lambda t, te, *_: (te[t], 0, 0)),
                  pl.BlockSpec((1, hid, d), lambda t, te, *_: (te[t], 0, 0))],
        out_specs=[pl.BlockSpec((tm * n_sub, LANES), lambda t, te, *_: (t, 0)),
                   pl.BlockSpec(memory_space=pltpu.SMEM), pl.BlockSpec(memory_space=pltpu.SMEM)],
        scratch_shapes=[pltpu.VMEM((tm * n_sub, LANES), F32)],
    )
    return pl.pallas_call(
        functools.partial(_expert_kernel, n_tok=n_tok, tm=tm),
        grid_spec=grid_spec,
        out_shape=[jax.ShapeDtypeStruct((rows * n_sub, LANES), F32),
                   jax.ShapeDtypeStruct((rows,), jnp.int32), jax.ShapeDtypeStruct((rows,), F32)],
        compiler_params=_cparams(("arbitrary",)),
    )(tile_e, n_tiles, pos, w_flat, h2t, wg, wu, wd)


def _combine_kernel(row_off_ref, row_w_ref, ys_ref, x1_ref, g2_ref, fng_ref, yc_ref, yl_ref, acc_ref, stage_ref,
                    *, t_max, ctx_tiles):
    step = pl.program_id(0)
    tm = row_off_ref.shape[2]
    tmf = x1_ref.shape[0]
    n_sub = x1_ref.shape[1] // LANES
    group = SUBLANES

    @pl.when(step == 0)
    def _():
        acc_ref[...] = jnp.zeros_like(acc_ref)

    @pl.when(step < t_max)
    def _():
        def body(ib, carry):
            slots, vals = [], []
            for k in range(group):
                i = ib * group + k
                slot = pl.ds(pl.multiple_of(row_off_ref[0, 0, i], n_sub), n_sub)
                slots.append(slot)
                vals.append(acc_ref[slot, :]
                            + row_w_ref[0, 0, i] * ys_ref[pl.ds(pl.multiple_of(i * n_sub, n_sub), n_sub), :])
            for slot, val in zip(slots, vals):
                acc_ref[slot, :] = val
            return carry

        lax.fori_loop(0, tm // group, body, 0)

    @pl.when(step >= t_max)
    def _():
        j = step - t_max
        stage_ref[...] = acc_ref[pl.ds(pl.multiple_of(j * tmf * n_sub, tmf * n_sub), tmf * n_sub), :]
        moe = jnp.concatenate([stage_ref[pl.ds(s, tmf, stride=n_sub), :] for s in range(n_sub)], axis=1)
        x2 = x1_ref[...] + g2_ref[0] * moe
        y = x2 * lax.rsqrt(jnp.mean(x2 * x2, axis=-1, keepdims=True) + EPS) * fng_ref[...]

        @pl.when(j < ctx_tiles)
        def _():
            yc_ref[...] = y

        @pl.when(j >= ctx_tiles)
        def _():
            yl_ref[...] = y


def _combine(ys, row_off, row_w, x1, mod3, fng, n_ctx, t_lat):
    n, d = x1.shape
    tm = MOE_TM
    tmf = MOE_TMF
    t_max = row_off.shape[0] // tm
    n_sub = d // LANES
    assert n_sub == SUBLANES, "one token must fill exactly one (8, 128) tile"
    ctx_tiles = n_ctx // tmf
    row = functools.partial(_mod_row, tm=tmf, n_ctx=n_ctx, t_lat=t_lat)
    src = lambda s: jnp.minimum(s, t_max - 1)
    dst = lambda s: jnp.maximum(s - t_max, 0)
    smem_tile = pl.BlockSpec((1, 1, tm), lambda s: (src(s), 0, 0), memory_space=pltpu.SMEM)
    return pl.pallas_call(
        functools.partial(_combine_kernel, t_max=t_max, ctx_tiles=ctx_tiles),
        grid=(t_max + n // tmf,),
        in_specs=[smem_tile, smem_tile,
                  pl.BlockSpec((tm * n_sub, LANES), lambda s: (src(s), 0)),
                  pl.BlockSpec((tmf, d), lambda s: (dst(s), 0)),
                  pl.BlockSpec((1, 1, d), lambda s: (row(dst(s)), 0, 5)),
                  pl.BlockSpec((1, d), lambda s: (0, 0))],
        out_specs=[pl.BlockSpec((tmf, d), lambda s: (jnp.minimum(dst(s), ctx_tiles - 1), 0)),
                   pl.BlockSpec((tmf, d), lambda s: (jnp.maximum(dst(s) - ctx_tiles, 0), 0))],
        out_shape=[jax.ShapeDtypeStruct((n_ctx, d), F32), jax.ShapeDtypeStruct((n - n_ctx, d), F32)],
        scratch_shapes=[pltpu.VMEM(((n + 1) * n_sub, LANES), F32),
                        pltpu.VMEM((tmf * n_sub, LANES), F32)],
        compiler_params=_cparams(("arbitrary",)),
    )(row_off.reshape(t_max, 1, tm), row_w.reshape(t_max, 1, tm), ys, x1, mod3, fng)


def _block_diag2(a, b):
    za = jnp.zeros((a.shape[0], b.shape[1]), a.dtype)
    zb = jnp.zeros((b.shape[0], a.shape[1]), a.dtype)
    return jnp.concatenate([jnp.concatenate([a, za], 1), jnp.concatenate([zb, b], 1)], 0)


def kernel(x_prompt, x_sample, state_rwkv, state_gla, c, c_ctx, w_mod, b_mod, norm1_g, norm2_g, w_in, shift_mu, a_w0, a_w_up, a_a0, a_a_up, a_g_up, a_k_k, a_k_a, a_r_k, a_ln_g, a_ln_b, b_gk_up, b_gk_bias, b_norm_g, w_out_a, w_out_b, w_o, w_router_group, b_router_group, w_router_expert, b_router_expert, w_exp_gate, w_exp_up, w_exp_down, final_norm_g):
    n_ctx_seq, t_ctx, d = x_prompt.shape
    n_lat_seq, t_lat, _ = x_sample.shape
    assert w_mod.shape[0] == 1, "single-layer configuration"
    n_ctx = n_ctx_seq * t_ctx
    n_lat = n_lat_seq * t_lat
    x = jnp.concatenate([x_prompt.reshape(n_ctx, d), x_sample.reshape(n_lat, d)], 0)

    cvec = jnp.concatenate([c_ctx[None], c, jnp.zeros((SUBLANES - 1 - n_lat_seq, d), F32)], 0)
    mod = _modulation(cvec, w_mod[0], b_mod[0][None])
    mod3 = mod[:, None, :]

    a_cols = shift_mu.shape[1]
    b_cols = 2 * B_HEADS * B_KEY + 2 * B_HEADS * B_VAL + 2 * b_gk_up.shape[2]
    b_pad = -b_cols % LANES
    w_in0 = w_in[0].astype(BF16)
    wa = w_in0[:, :a_cols]
    wb = jnp.pad(w_in0[:, a_cols:a_cols + b_cols], ((0, 0), (0, b_pad)))
    wg = w_in0[:, a_cols + b_cols:]
    pa, pb, pg = _in_proj(x, mod3, norm1_g, wa, wb, wg, n_ctx, t_lat)

    seg = jnp.kron(jnp.eye(A_HEADS, dtype=F32), jnp.ones((A_DIM, A_DIM), F32))
    consts_a = (shift_mu, _block_diag2(a_w_up[0, 0], a_w_up[0, 1]), _block_diag2(a_a_up[0, 0], a_a_up[0, 1]),
                a_g_up[0], a_w0[0], a_a0[0], a_k_k, a_k_a, a_r_k[0].reshape(1, A_WIDTH), seg)
    prep_c = _prep_a(pa, 0, n_ctx_seq, t_ctx, False, consts_a)
    ra, wk, bv, bonus, g_a = _prep_a(pa, n_ctx, n_lat_seq, t_lat, True, consts_a, prev_outs=prep_c)

    n_tok = n_ctx + n_lat
    assert (n_ctx // t_lat) % n_lat_seq == 0
    view3 = lambda a, t: a.reshape(n_tok // t, t * A_HEADS, LANES)
    view4 = lambda a, t: a.reshape(2, n_tok // t, t * A_HEADS, LANES)
    o_ctx, sfin = _scan_ctx(view3(ra, t_ctx), view4(wk, t_ctx), view4(bv, t_ctx), n_ctx_seq, t_ctx, 16)
    o_ctx = o_ctx.reshape(2, t_ctx, A_DIM, n_ctx_seq, A_HEADS).transpose(0, 3, 1, 4, 2).reshape(2, n_ctx, A_WIDTH)
    new_state_rwkv = sfin.reshape(2, A_DIM, A_DIM, n_ctx_seq, A_HEADS).transpose(3, 0, 4, 2, 1)[:, None]

    vl = A_DIM // 2
    s0 = state_rwkv[:, 0].reshape(n_lat_seq, 2, A_HEADS, 2, vl, A_DIM)
    s0 = s0.transpose(5, 4, 3, 1, 0, 2).reshape(A_DIM, vl, LANES)
    of_l, ob_l = _scan_lat(view3(ra, t_lat), view4(wk, t_lat), view4(bv, t_lat), s0, n_lat_seq, t_lat, 16,
                           n_ctx // t_lat // n_lat_seq)

    def lat_rows(o, dirn):
        o = o.reshape(t_lat, vl, 2, 2, n_lat_seq, A_HEADS)[:, :, :, dirn]
        return o.transpose(3, 0, 4, 2, 1).reshape(n_lat, A_WIDTH)

    o_f = jnp.concatenate([o_ctx[0], lat_rows(of_l, 0)], 0)
    o_b = jnp.concatenate([o_ctx[1], lat_rows(ob_l, 1)], 0)

    rank = b_gk_up.shape[2]
    wgk = jnp.zeros((2, LANES, B_HEADS * B_KEY), F32)
    wgk = wgk.at[0, :rank].set(b_gk_up[0, 0]).at[1, rank:2 * rank].set(b_gk_up[0, 1])
    zero_gla = jnp.zeros((n_ctx_seq, 1, 2, B_HEADS, B_KEY, B_VAL), F32)
    yb_c, new_state_gla = _gla(pb, 0, n_ctx_seq, t_ctx, wgk, b_gk_bias[0], b_norm_g, zero_gla, False)
    yb, _ = _gla(pb, n_ctx, n_lat_seq, t_lat, wgk, b_gk_bias[0], b_norm_g, state_gla, True, yb_prev=yb_c)

    w_r = jnp.concatenate([w_router_group[0], w_router_expert[0].transpose(1, 0, 2).reshape(d, N_EXPERTS)], 1)
    w_r = jnp.pad(w_r, ((0, 0), (0, LANES - w_r.shape[1])))
    b_r = jnp.concatenate([b_router_group[0], b_router_expert[0].reshape(N_EXPERTS)])
    b_r = jnp.pad(b_r, (0, LANES - b_r.shape[0]))[None]
    consts_m = (a_ln_g, a_ln_b, seg, w_out_a[0].astype(BF16), w_out_b[0].astype(BF16), w_o[0].astype(BF16),
                norm2_g, w_r, b_r)
    x1, h2t, route = _mix(o_f, o_b, bonus, g_a, yb, pg, x, mod3, consts_m, n_ctx, t_lat)

    pos, w_flat, tile_e, n_tiles, t_max = _route_tables(route, n_tok)
    ys, row_off, row_w = _experts(h2t, pos, w_flat, tile_e, n_tiles, t_max, w_exp_gate[0].astype(BF16),
                                  w_exp_up[0].astype(BF16), w_exp_down[0].astype(BF16), n_tok)
    y_c, y_l = _combine(ys, row_off, row_w, x1, mod3, final_norm_g[None], n_ctx, t_lat)
    return (y_c.reshape(n_ctx_seq, t_ctx, d), y_l.reshape(n_lat_seq, t_lat, d), new_state_rwkv, new_state_gla)
```

```python
import functools

import jax
import jax.numpy as jnp
from jax import lax
from jax.experimental import pallas as pl
from jax.experimental.pallas import tpu as pltpu

F32 = jnp.float32
BF16 = jnp.bfloat16
HI = lax.Precision.HIGHEST

LANES = 128
SUBLANES = 8
VMEM_LIMIT = 56 * 1024 * 1024

EPS = 1e-6
GN_EPS = 64e-5
GRID_W = 64
A_HEADS, A_DIM = 8, 64
A_WIDTH = A_HEADS * A_DIM
B_HEADS, B_KEY, B_VAL = 4, 64, 128
GLA_CHUNK = 64
GLA_GATE_NORM = 16.0
N_GROUPS, N_PER_GROUP = 4, 8
N_EXPERTS = N_GROUPS * N_PER_GROUP


def _cparams(sem, flags=None):
    return pltpu.CompilerParams(dimension_semantics=sem, vmem_limit_bytes=VMEM_LIMIT, flags=flags)


def _log_sigmoid(z):
    return jnp.minimum(z, 0.0) - jnp.log(1.0 + jnp.exp(-jnp.abs(z)))


def _dot(a, b, precision=None):
    return jnp.dot(a, b, preferred_element_type=F32, precision=precision)


def _bf16_terms(x, n_terms):
    terms = []
    for _ in range(n_terms - 1):
        t = x.astype(BF16)
        terms.append(t)
        x = x - t.astype(F32)
    terms.append(x.astype(BF16))
    return terms


def _dot_lhs_terms(a, b_exact, n_terms):
    out = None
    for t in _bf16_terms(a, n_terms):
        part = _dot(t, b_exact)
        out = part if out is None else out + part
    return out


def _dot_x3(a, b):
    a_hi, a_lo = _bf16_terms(a, 2)
    b_hi, b_lo = _bf16_terms(b, 2)
    return _dot(a_hi, b_hi) + (_dot(a_lo, b_hi) + _dot(a_hi, b_lo))


def _mod_kernel(c_ref, w_ref, b_ref, o_ref):
    c = c_ref[...]
    o_ref[...] = _dot(c * jax.nn.sigmoid(c), w_ref[...], HI) + b_ref[...]


def _modulation(cvec, w_mod, b_mod):
    rows, d = cvec.shape
    n = w_mod.shape[1]
    tn = n // 4
    return pl.pallas_call(
        _mod_kernel,
        grid=(n // tn,),
        in_specs=[pl.BlockSpec((rows, d), lambda j: (0, 0)),
                  pl.BlockSpec((d, tn), lambda j: (0, j)),
                  pl.BlockSpec((1, tn), lambda j: (0, j))],
        out_specs=pl.BlockSpec((rows, tn), lambda j: (0, j)),
        out_shape=jax.ShapeDtypeStruct((rows, n), F32),
        compiler_params=_cparams(("arbitrary",)),
    )(cvec, w_mod, b_mod)


def _mod_row(i, tm, n_ctx, t_lat):
    start = i * tm
    return jnp.where(start < n_ctx, 0, 1 + (start - n_ctx) // t_lat)


def _in_proj_kernel(x_ref, sh_ref, sc_ref, g_ref, wa_ref, wb_ref, wg_ref, pa_ref, pb_ref, pg_ref):
    x = x_ref[...]
    y = x * lax.rsqrt(jnp.mean(x * x, axis=-1, keepdims=True) + EPS)
    h = ((y * g_ref[...]) * (1.0 + sc_ref[0]) + sh_ref[0]).astype(BF16)
    pa_ref[...] = _dot(h, wa_ref[...])
    pb_ref[...] = _dot(h, wb_ref[...])
    pg_ref[...] = _dot(h, wg_ref[...])


def _in_proj(x, mod3, norm_g, wa, wb, wg, n_ctx, t_lat):
    n, d = x.shape
    tm = 256
    row = functools.partial(_mod_row, tm=tm, n_ctx=n_ctx, t_lat=t_lat)
    full = lambda w: pl.BlockSpec(w.shape, lambda i: (0, 0))
    outs = [jax.ShapeDtypeStruct((n, w.shape[1]), F32) for w in (wa, wb, wg)]
    return pl.pallas_call(
        _in_proj_kernel,
        grid=(n // tm,),
        in_specs=[pl.BlockSpec((tm, d), lambda i: (i, 0)),
                  pl.BlockSpec((1, 1, d), lambda i: (row(i), 0, 0)),
                  pl.BlockSpec((1, 1, d), lambda i: (row(i), 0, 1)),
                  pl.BlockSpec((1, d), lambda i: (0, 0)),
                  full(wa), full(wb), full(wg)],
        out_specs=[pl.BlockSpec((tm, w.shape[1]), lambda i: (i, 0)) for w in (wa, wb, wg)],
        out_shape=outs,
        compiler_params=_cparams(("parallel",)),
    )(x, mod3, mod3, norm_g, wa, wb, wg)


def _pair_heads(x, y):
    lane = lax.broadcasted_iota(jnp.int32, (1, LANES), 1)
    low = lane < A_DIM
    tiles = []
    for p in range(A_HEADS // 2):
        xv = x[:, p * LANES:(p + 1) * LANES]
        yv = y[:, p * LANES:(p + 1) * LANES]
        tiles.append(jnp.where(low, xv, pltpu.roll(yv, A_DIM, axis=1)))
        tiles.append(jnp.where(low, pltpu.roll(xv, A_DIM, axis=1), yv))
    return tiles


def _prep_a_kernel(pa_ref, prev_ref, next_ref, mu_ref, wup_ref, aup_ref, gup_ref, w0_ref, a0_ref, kk_ref, ka_ref,
                   rk_ref, seg_ref, ra_ref, wk_ref, bv_ref, bonus_ref, g_ref, *, on_grid, seq_len):
    pa = pa_ref[...]
    t_len, cols = pa.shape
    halo = prev_ref.shape[0]
    ext = jnp.concatenate([prev_ref[...], pa, next_ref[...]], axis=0)
    col = lax.broadcasted_iota(jnp.int32, (1, cols), 1)
    row = lax.broadcasted_iota(jnp.int32, (t_len, 1), 0) + pl.program_id(1) * t_len
    quarter = col // (cols // 4)

    def shifted(s):
        if s % SUBLANES == 0:
            moved = ext[halo - s:halo - s + t_len]
        else:
            moved = pltpu.roll(ext, s % ext.shape[0], axis=0)[halo:halo + t_len]
        valid = (row >= s) & (row < seq_len + s)
        return jnp.where(valid, moved, 0.0)

    if on_grid:
        gcol = row % GRID_W
        left = jnp.where(gcol != 0, shifted(1), 0.0)
        right = jnp.where(gcol != GRID_W - 1, shifted(-1), 0.0)
        nb = jnp.where(quarter == 0, left,
                       jnp.where(quarter == 1, right,
                                 jnp.where(quarter == 2, shifted(GRID_W), shifted(-GRID_W))))
    else:
        nb = jnp.where(quarter % 2 == 0, shifted(1), shifted(-1))
    pa = pa + (nb - pa) * mu_ref[...]

    r = pa[:, 0:A_WIDTH]
    k = pa[:, A_WIDTH:2 * A_WIDTH]
    v = pa[:, 2 * A_WIDTH:3 * A_WIDTH]
    c0 = 3 * A_WIDTH
    wd = pa[:, c0:c0 + LANES]
    ad = pa[:, c0 + LANES:c0 + 2 * LANES]
    gd = pa[:, c0 + 2 * LANES:c0 + 3 * LANES]
    seg = seg_ref[...]

    up_w = _dot_x3(jnp.tanh(wd), wup_ref[...])
    up_a = _dot_x3(ad, aup_ref[...])
    g_ref[...] = _dot_x3(jax.nn.sigmoid(gd), gup_ref[...])

    kk = k * kk_ref[...]
    kk = kk / jnp.maximum(jnp.sqrt(_dot_lhs_terms(kk * kk, seg, 2)), 1e-12)
    neg_kk = -kk
    kd_sum = None
    for d in range(2):
        sl = slice(d * A_WIDTH, (d + 1) * A_WIDTH)
        w_log = _log_sigmoid(w0_ref[d:d + 1, :] + up_w[:, sl]) - 0.5
        decay = jnp.exp(-jnp.exp(w_log))
        a = jax.nn.sigmoid(a0_ref[d:d + 1, :] + up_a[:, sl])
        kd = k * (1.0 + (a - 1.0) * ka_ref[...])
        kd_sum = kd if kd_sum is None else kd_sum + kd
        for h, tile in enumerate(_pair_heads(decay, kd)):
            wk_ref[d, pl.ds(h, t_len, stride=A_HEADS), :] = tile
        for h, tile in enumerate(_pair_heads(kk * a, v)):
            bv_ref[d, pl.ds(h, t_len, stride=A_HEADS), :] = tile
    for h, tile in enumerate(_pair_heads(r, neg_kk)):
        ra_ref[pl.ds(h, t_len, stride=A_HEADS), :] = tile
    bonus_ref[...] = _dot_lhs_terms(r * kd_sum * rk_ref[...], seg, 2) * v


def _skip_refs(fn, start, count):
    return lambda *refs, **kw: fn(*refs[:start], *refs[start + count:], **kw)


def _prep_a(pa, row0, n_seq, t_len, on_grid, consts, prev_outs=None):
    n_rows, cols = pa.shape
    tt = 256
    halo = GRID_W if on_grid else SUBLANES
    tiles = t_len // tt
    blk0 = row0 // tt
    main = lambda i, j: (blk0 + i * tiles + j, 0)
    prev = lambda i, j: (jnp.maximum((blk0 + i * tiles + j) * (tt // halo) - 1, 0), 0)
    nxt = lambda i, j: (jnp.minimum((blk0 + i * tiles + j + 1) * (tt // halo), n_rows // halo - 1), 0)
    full = lambda w: pl.BlockSpec(w.shape, lambda i, j: (0,) * w.ndim)
    hr = jax.ShapeDtypeStruct((n_rows * A_HEADS, LANES), F32)
    hr2 = jax.ShapeDtypeStruct((2, n_rows * A_HEADS, LANES), F32)
    tok = jax.ShapeDtypeStruct((n_rows, A_WIDTH), F32)
    hr_spec = pl.BlockSpec((tt * A_HEADS, LANES), main)
    hr2_spec = pl.BlockSpec((2, tt * A_HEADS, LANES), lambda i, j: (0,) + main(i, j))
    tok_spec = pl.BlockSpec((tt, A_WIDTH), main)
    kern = functools.partial(_prep_a_kernel, on_grid=on_grid, seq_len=t_len)
    in_specs = [pl.BlockSpec((tt, cols), main), pl.BlockSpec((halo, cols), prev),
                pl.BlockSpec((halo, cols), nxt)] + [full(w) for w in consts]
    args = [pa, pa, pa, *consts]
    aliases = {}
    if prev_outs is not None:
        n_in = len(args)
        kern = _skip_refs(kern, n_in, len(prev_outs))
        aliases = {n_in + k: k for k in range(len(prev_outs))}
        in_specs += [pl.BlockSpec(memory_space=pl.ANY)] * len(prev_outs)
        args += list(prev_outs)
    return pl.pallas_call(
        kern,
        grid=(n_seq, tiles),
        in_specs=in_specs,
        out_specs=[hr_spec, hr2_spec, hr2_spec, tok_spec, tok_spec],
        out_shape=[hr, hr2, hr2, tok, tok],
        input_output_aliases=aliases,
        compiler_params=_cparams(("parallel", "parallel")),
    )(*args)


SCAN_SETS = 4


def _scan_block(load_tile, store_o, s_ref, kra, kwk, kbv, tb, v_rows, lane_rep):
    n_sets = kra.shape[0]
    ahead = 2
    assert tb % n_sets == 0 and n_sets > ahead

    def transpose_into(slot, j):
        kra[slot] = load_tile(0, j).T
        kwk[slot] = load_tile(1, j).T
        kbv[slot] = load_tile(2, j).T

    for j in range(ahead):
        transpose_into(j, j)

    def row_b(buf, slot, r):
        return jnp.broadcast_to(buf[slot, pl.ds(r, 1), :], (SUBLANES, LANES))

    n_kb = A_DIM // SUBLANES
    n_vb = v_rows // SUBLANES
    vrows = lambda vb: pl.ds(vb * SUBLANES, SUBLANES)

    def sa_init(kb, acc):
        acc = list(acc)
        for kk in range(SUBLANES):
            k = kb * SUBLANES + kk
            ab = row_b(kra, 0, A_DIM + k)
            for vb in range(n_vb):
                acc[vb] = acc[vb] + s_ref[k, vrows(vb), :] * ab
        return tuple(acc)

    zeros = tuple(jnp.zeros((SUBLANES, LANES), F32) for _ in range(n_vb))
    sa0 = lax.fori_loop(0, n_kb, sa_init, zeros)
    lane = lax.broadcasted_iota(jnp.int32, (1, LANES), 1)

    def step(j, slot, sa):
        transpose_into((slot + ahead) % n_sets, jnp.minimum(j + ahead, tb - 1))
        nxt = (slot + 1) % n_sets
        vt = kbv[slot, pl.ds(A_DIM, A_DIM), :]
        if lane_rep == 1:
            v = vt
        else:
            v = jnp.where(lane < LANES // 2, vt[:v_rows], vt[v_rows:])
        v = [v[vb * SUBLANES:(vb + 1) * SUBLANES] for vb in range(n_vb)]
        o = list(zeros)
        sa_next = list(zeros)
        for k in range(A_DIM):
            wb, kb = row_b(kwk, slot, k), row_b(kwk, slot, A_DIM + k)
            bb, rb = row_b(kbv, slot, k), row_b(kra, slot, k)
            ab = row_b(kra, nxt, A_DIM + k)
            for vb in range(n_vb):
                s_new = s_ref[k, vrows(vb), :] * wb + kb * v[vb] + bb * sa[vb]
                s_ref[k, vrows(vb), :] = s_new
                o[vb] = o[vb] + s_new * rb
                sa_next[vb] = sa_next[vb] + s_new * ab
        store_o(j, jnp.concatenate(o, axis=0))
        return tuple(sa_next)

    def rotation(jj, sa):
        for slot in range(n_sets):
            sa = step(jj * n_sets + slot, slot, sa)
        return sa

    lax.fori_loop(0, tb // n_sets, rotation, sa0)


def _scan_ctx_kernel(ra_ref, wk_ref, bv_ref, o_ref, sfin_ref, s_ref, kra, kwk, kbv, *, tb, n_seq):
    g = pl.program_id(0)
    i = pl.program_id(1)

    @pl.when(i == 0)
    def _():
        s_ref[...] = jnp.zeros_like(s_ref)

    def local_t(j):
        return j + g * (tb - 1 - 2 * j)

    refs = (ra_ref, wk_ref, bv_ref)

    def load_tile(which, j):
        ref = refs[which]
        rows = pl.ds(pl.multiple_of(local_t(j) * A_HEADS, A_HEADS), A_HEADS)
        slabs = [ref[b, rows, :] if which == 0 else ref[0, b, rows, :] for b in range(n_seq)]
        return jnp.concatenate(slabs, axis=0)

    def store_o(j, o):
        o_ref[0, local_t(j)] = o

    _scan_block(load_tile, store_o, s_ref, kra, kwk, kbv, tb, A_DIM, 1)

    @pl.when(i == pl.num_programs(1) - 1)
    def _():
        sfin_ref[0] = s_ref[...]


def _scan_ctx(ra, wk, bv, n_seq, t_len, tb):
    nb = t_len // tb
    blk = lambda g, i: jnp.where(g == 0, i, nb - 1 - i)
    rows = tb * A_HEADS
    return pl.pallas_call(
        functools.partial(_scan_ctx_kernel, tb=tb, n_seq=n_seq),
        grid=(2, nb),
        in_specs=[pl.BlockSpec((n_seq, rows, LANES), lambda g, i: (0, blk(g, i), 0)),
                  pl.BlockSpec((1, n_seq, rows, LANES), lambda g, i: (g, 0, blk(g, i), 0)),
                  pl.BlockSpec((1, n_seq, rows, LANES), lambda g, i: (g, 0, blk(g, i), 0))],
        out_specs=[pl.BlockSpec((1, tb, A_DIM, LANES), lambda g, i: (g, blk(g, i), 0, 0)),
                   pl.BlockSpec((1, A_DIM, A_DIM, LANES), lambda g, i: (g, 0, 0, 0))],
        out_shape=[jax.ShapeDtypeStruct((2, t_len, A_DIM, LANES), F32),
                   jax.ShapeDtypeStruct((2, A_DIM, A_DIM, LANES), F32)],
        scratch_shapes=[pltpu.VMEM((A_DIM, A_DIM, LANES), F32),
                        pltpu.VMEM((SCAN_SETS, LANES, LANES), F32),
                        pltpu.VMEM((SCAN_SETS, LANES, LANES), F32),
                        pltpu.VMEM((SCAN_SETS, LANES, LANES), F32)],
        compiler_params=_cparams(("arbitrary", "arbitrary")),
    )(ra, wk, bv)


def _scan_lat_kernel(raf_ref, rab_ref, wkf_ref, wkb_ref, bvf_ref, bvb_ref, s0_ref, of_ref, ob_ref,
                     s_ref, kra, kwk, kbv, *, tb, n_seq):
    i = pl.program_id(0)

    @pl.when(i == 0)
    def _():
        s_ref[...] = s0_ref[...]

    refs = ((raf_ref, rab_ref), (wkf_ref, wkb_ref), (bvf_ref, bvb_ref))

    def load_tile(which, j):
        fwd, bwd = refs[which]
        rows_f = pl.ds(pl.multiple_of(j * A_HEADS, A_HEADS), A_HEADS)
        rows_b = pl.ds(pl.multiple_of((tb - 1 - j) * A_HEADS, A_HEADS), A_HEADS)
        if which == 0:
            half = ([fwd[b, rows_f, :] for b in range(n_seq)] + [bwd[b, rows_b, :] for b in range(n_seq)])
        else:
            half = ([fwd[0, b, rows_f, :] for b in range(n_seq)] + [bwd[0, b, rows_b, :] for b in range(n_seq)])
        return jnp.concatenate(half + half, axis=0)

    def store_o(j, o):
        of_ref[j] = o
        ob_ref[tb - 1 - j] = o

    _scan_block(load_tile, store_o, s_ref, kra, kwk, kbv, tb, A_DIM // 2, 2)


def _scan_lat(ra, wk, bv, s0, n_seq, t_len, tb, ub):
    nb = t_len // tb
    rows = tb * A_HEADS
    vl = A_DIM // 2
    f3 = pl.BlockSpec((n_seq, rows, LANES), lambda i: (ub, i, 0))
    b3 = pl.BlockSpec((n_seq, rows, LANES), lambda i: (ub, nb - 1 - i, 0))
    f4 = lambda d: pl.BlockSpec((1, n_seq, rows, LANES), lambda i: (d, ub, i, 0))
    b4 = lambda d: pl.BlockSpec((1, n_seq, rows, LANES), lambda i: (d, ub, nb - 1 - i, 0))
    o_shape = jax.ShapeDtypeStruct((t_len, vl, LANES), F32)
    return pl.pallas_call(
        functools.partial(_scan_lat_kernel, tb=tb, n_seq=n_seq),
        grid=(nb,),
        in_specs=[f3, b3, f4(0), b4(1), f4(0), b4(1),
                  pl.BlockSpec((A_DIM, vl, LANES), lambda i: (0, 0, 0))],
        out_specs=[pl.BlockSpec((tb, vl, LANES), lambda i: (i, 0, 0)),
                   pl.BlockSpec((tb, vl, LANES), lambda i: (nb - 1 - i, 0, 0))],
        out_shape=[o_shape, o_shape],
        scratch_shapes=[pltpu.VMEM((A_DIM, vl, LANES), F32),
                        pltpu.VMEM((SCAN_SETS, LANES, LANES), F32),
                        pltpu.VMEM((SCAN_SETS, LANES, LANES), F32),
                        pltpu.VMEM((SCAN_SETS, LANES, LANES), F32)],
        compiler_params=_cparams(("arbitrary",)),
    )(ra, ra, wk, wk, bv, bv, s0)


def _gla_kernel(pb_ref, wgk_ref, bgk_ref, ng_ref, s0_ref, yb_ref, sfin_ref, st_scr, b_scr, of_scr, ob_scr, *,
                has_init):
    t_len = pb_ref.shape[0]
    c_len = GLA_CHUNK
    n_chunks = t_len // c_len
    qw = B_HEADS * B_KEY
    vw = B_HEADS * B_VAL
    grp = 4 * c_len
    ri = lax.broadcasted_iota(jnp.int32, (c_len, c_len), 0)
    ci = lax.broadcasted_iota(jnp.int32, (c_len, c_len), 1)
    tri = (ri >= ci, ri <= ci)
    gr = lax.broadcasted_iota(jnp.int32, (grp, grp), 0)
    gc = lax.broadcasted_iota(jnp.int32, (grp, grp), 1)
    same_chunk = (gr // c_len) == (gc // c_len)
    tri_grp = ((same_chunk & (gr >= gc)).astype(BF16), (same_chunk & (gr <= gc)).astype(BF16))
    lane = lax.broadcasted_iota(jnp.int32, (1, LANES), 1)
    half_mask = (lane < B_KEY, lane >= B_KEY)

    gdt = pb_ref[:, 2 * qw + 2 * vw:2 * qw + 2 * vw + LANES]
    for d in range(2):
        gk = _log_sigmoid(_dot_x3(gdt, wgk_ref[d]) + bgk_ref[d:d + 1, :]) / GLA_GATE_NORM
        for g in range(t_len // grp):
            acc = None
            for term in _bf16_terms(gk[g * grp:(g + 1) * grp], 3):
                part = _dot(tri_grp[d], term)
                acc = part if acc is None else acc + part
            b_scr[d, g * grp:(g + 1) * grp, :] = acc

    zero_half = jnp.zeros((B_KEY, B_VAL), F32)
    for d in range(2):
        for h in range(B_HEADS):
            if has_init:
                s0 = s0_ref[0, 0, d, h]
                full = jnp.concatenate([s0, zero_half] if h % 2 == 0 else [zero_half, s0], axis=0)
                st_scr[d, h] = full.T
            else:
                st_scr[d, h] = jnp.zeros((B_VAL, LANES), F32)

    def chunk(c, d):
        rows = pl.ds(pl.multiple_of(c * c_len, c_len), c_len)
        o_scr = of_scr if d == 0 else ob_scr
        b = b_scr[d, rows, :]
        for p in range(B_HEADS // 2):
            sl = slice(p * LANES, (p + 1) * LANES)
            bp = b[:, sl]
            tot_row = bp[c_len - 1:c_len, :] if d == 0 else bp[0:1, :]
            qe = pb_ref[rows, p * LANES:(p + 1) * LANES] * (B_KEY ** -0.5) * jnp.exp(bp)
            kp = pb_ref[rows, qw + p * LANES:qw + (p + 1) * LANES]
            ke = (kp * jnp.exp(-bp)).astype(BF16)
            kd = kp * jnp.exp(tot_row - bp)
            decay_row = jnp.exp(tot_row)
            for hh in range(2):
                h = 2 * p + hh
                v = pb_ref[rows, 2 * qw + h * B_VAL:2 * qw + (h + 1) * B_VAL].astype(BF16)
                qm = jnp.where(half_mask[hh], qe, 0.0).astype(BF16)
                km = jnp.where(half_mask[hh], kd, 0.0).astype(BF16)
                att = lax.dot_general(qm, ke, (((1,), (1,)), ((), ())), preferred_element_type=F32)
                att = jnp.where(tri[d], att, 0.0).astype(BF16)
                st = st_scr[d, h]
                inter = lax.dot_general(qm, st.astype(BF16), (((1,), (1,)), ((), ())), preferred_element_type=F32)
                o_scr[rows, h * B_VAL:(h + 1) * B_VAL] = _dot(att, v) + inter
                vk = lax.dot_general(v, km, (((0,), (0,)), ((), ())), preferred_element_type=F32)
                st_scr[d, h] = st * decay_row + vk

    def both_directions(cc, carry):
        chunk(cc, 0)
        chunk(n_chunks - 1 - cc, 1)
        return carry

    lax.fori_loop(0, n_chunks, both_directions, 0)

    def finish(c, carry):
        rows = pl.ds(pl.multiple_of(c * c_len, c_len), c_len)
        for h in range(B_HEADS):
            sl = slice(h * B_VAL, (h + 1) * B_VAL)
            o = of_scr[rows, sl] + ob_scr[rows, sl]
            o = o * lax.rsqrt(jnp.mean(o * o, axis=-1, keepdims=True) + EPS) * ng_ref[...]
            gate = pb_ref[rows, 2 * qw + vw + h * B_VAL:2 * qw + vw + (h + 1) * B_VAL]
            yb_ref[rows, sl] = o * (gate * jax.nn.sigmoid(gate))
        return carry

    lax.fori_loop(0, n_chunks, finish, 0)

    for d in range(2):
        for h in range(B_HEADS):
            half = (h % 2) * B_KEY
            sfin_ref[0, 0, d, h] = st_scr[d, h].T[half:half + B_KEY, :]


def _gla(pb, row0, n_seq, t_len, wgk, bgk, ng, s0, has_init, yb_prev=None):
    n_rows, cols = pb.shape
    blk0 = row0 // t_len
    st_shape = (n_seq, 1, 2, B_HEADS, B_KEY, B_VAL)
    st_spec = pl.BlockSpec((1, 1, 2, B_HEADS, B_KEY, B_VAL), lambda i: (i, 0, 0, 0, 0, 0))
    full = lambda w: pl.BlockSpec(w.shape, lambda i: (0,) * w.ndim)
    kern = functools.partial(_gla_kernel, has_init=has_init)
    in_specs = [pl.BlockSpec((t_len, cols), lambda i: (i + blk0, 0)), full(wgk), full(bgk), full(ng), st_spec]
    args = [pb, wgk, bgk, ng, s0]
    aliases = {}
    if yb_prev is not None:
        kern = _skip_refs(kern, len(args), 1)
        aliases = {len(args): 0}
        in_specs.append(pl.BlockSpec(memory_space=pl.ANY))
        args.append(yb_prev)
    return pl.pallas_call(
        kern,
        grid=(n_seq,),
        in_specs=in_specs,
        out_specs=[pl.BlockSpec((t_len, B_HEADS * B_VAL), lambda i: (i + blk0, 0)), st_spec],
        out_shape=[jax.ShapeDtypeStruct((n_rows, B_HEADS * B_VAL), F32),
                   jax.ShapeDtypeStruct(st_shape, F32)],
        input_output_aliases=aliases,
        scratch_shapes=[pltpu.VMEM((2, B_HEADS, B_VAL, LANES), F32),
                        pltpu.VMEM((2, t_len, B_HEADS * B_KEY), F32),
                        pltpu.VMEM((t_len, B_HEADS * B_VAL), F32),
                        pltpu.VMEM((t_len, B_HEADS * B_VAL), F32)],
        compiler_params=_cparams(("parallel",)),
    )(*args)


def _mix_kernel(of_ref, ob_ref, bonus_ref, g_ref, yb_ref, pg_ref, x_ref, g1_ref, sh2_ref, sc2_ref,
                lng_ref, lnb_ref, seg_ref, woa_ref, wob_ref, wo_ref, n2g_ref, wr_ref, br_ref,
                x1_ref, h2_ref, route_ref):
    d_model = x_ref.shape[1]
    seg = seg_ref[...]
    o = of_ref[...] + ob_ref[...]
    mu = _dot_lhs_terms(o, seg, 2) * (1.0 / A_DIM)
    dev = o - mu
    var = _dot_lhs_terms(dev * dev, seg, 2) * (1.0 / A_DIM)
    on = dev * lax.rsqrt(var + GN_EPS) * lng_ref[...] + lnb_ref[...]
    ya = ((on + bonus_ref[...]) * g_ref[...]).astype(BF16)
    ta = _dot(ya, woa_ref[...])
    tb = _dot(yb_ref[...].astype(BF16), wob_ref[...])
    pg = pg_ref[...]
    u = jax.nn.sigmoid(pg[:, :d_model]) * ta + jax.nn.sigmoid(pg[:, d_model:]) * tb
    mix = _dot(u.astype(BF16), wo_ref[...])
    x1 = x_ref[...] + g1_ref[0] * mix
    x1_ref[...] = x1
    y = x1 * lax.rsqrt(jnp.mean(x1 * x1, axis=-1, keepdims=True) + EPS)
    h2 = (y * n2g_ref[...]) * (1.0 + sc2_ref[0]) + sh2_ref[0]
    for s in range(d_model // LANES):
        h2_ref[pl.ds(s, h2.shape[0], stride=SUBLANES), :] = h2[:, s * LANES:(s + 1) * LANES]

    logits = _dot_x3(h2, wr_ref[...]) + br_ref[...]
    lane = lax.broadcasted_iota(jnp.int32, logits.shape, 1)
    neg = jnp.float32(-jnp.inf)
    big = jnp.int32(1 << 20)
    is_g = lane < N_GROUPS
    gl = jnp.where(is_g, logits, neg)
    gmax = jnp.max(gl, axis=-1, keepdims=True)
    grp = jnp.min(jnp.where(gl == gmax, lane, big), axis=-1, keepdims=True)
    g_w = 1.0 / jnp.sum(jnp.where(is_g, jnp.exp(gl - gmax), 0.0), axis=-1, keepdims=True)
    lo = N_GROUPS + grp * N_PER_GROUP
    in_grp = (lane >= lo) & (lane < lo + N_PER_GROUP)
    el = jnp.where(in_grp, logits, neg)
    m1 = jnp.max(el, axis=-1, keepdims=True)
    i1 = jnp.min(jnp.where(el == m1, lane, big), axis=-1, keepdims=True)
    el2 = jnp.where(lane == i1, neg, el)
    m2 = jnp.max(el2, axis=-1, keepdims=True)
    i2 = jnp.min(jnp.where(el2 == m2, lane, big), axis=-1, keepdims=True)
    e2 = jnp.exp(m2 - m1)
    w1 = g_w / (1.0 + e2)
    w2 = g_w * e2 / (1.0 + e2)
    e1 = (i1 - N_GROUPS).astype(F32)
    e2_id = (i2 - N_GROUPS).astype(F32)
    route_ref[...] = jnp.where(lane == 0, e1, jnp.where(lane == 1, e2_id, jnp.where(lane == 2, w1, w2)))


def _mix(of, ob, bonus, g, yb, pg, x, mod3, consts, n_ctx, t_lat):
    n, d = x.shape
    tm = 256
    row = functools.partial(_mod_row, tm=tm, n_ctx=n_ctx, t_lat=t_lat)
    full = lambda w: pl.BlockSpec(w.shape, lambda i: (0,) * w.ndim)
    tile = lambda a: pl.BlockSpec((tm, a.shape[1]), lambda i: (i, 0))
    mod_spec = lambda c: pl.BlockSpec((1, 1, d), lambda i: (row(i), 0, c))
    return pl.pallas_call(
        _mix_kernel,
        grid=(n // tm,),
        in_specs=[tile(of), tile(ob), tile(bonus), tile(g), tile(yb), tile(pg), tile(x),
                  mod_spec(2), mod_spec(3), mod_spec(4)] + [full(w) for w in consts],
        out_specs=[pl.BlockSpec((tm, d), lambda i: (i, 0)),
                   pl.BlockSpec((tm * d // LANES, LANES), lambda i: (i, 0)),
                   pl.BlockSpec((tm, LANES), lambda i: (i, 0))],
        out_shape=[jax.ShapeDtypeStruct((n, d), F32), jax.ShapeDtypeStruct((n * d // LANES, LANES), F32),
                   jax.ShapeDtypeStruct((n, LANES), F32)],
        compiler_params=_cparams(("parallel",)),
    )(of, ob, bonus, g, yb, pg, x, mod3, mod3, mod3, *consts)


MOE_TM = 256
MOE_TMF = 256


def _route_tables(route, n_tok):
    e = route[:, 0:2].astype(jnp.int32).reshape(-1)
    w = route[:, 2:4].reshape(-1)
    onehot = (e[:, None] == jnp.arange(N_EXPERTS, dtype=jnp.int32)[None]).astype(jnp.int32)
    csum = jnp.cumsum(onehot, axis=0)
    rank = jnp.sum(csum * onehot, axis=1) - 1
    tiles = (csum[-1] + MOE_TM - 1) // MOE_TM
    tile_end = jnp.cumsum(tiles)
    tile_start = tile_end - tiles
    pos = jnp.sum(onehot * tile_start[None, :], axis=1) * MOE_TM + rank
    t_max = 2 * n_tok // MOE_TM + N_EXPERTS
    tile_e = jnp.sum((tile_end[None, :] <= jnp.arange(t_max, dtype=jnp.int32)[:, None]).astype(jnp.int32), axis=1)
    tile_e = jnp.minimum(tile_e, N_EXPERTS - 1)
    return pos, w, tile_e, tile_end[-1:], t_max


def _expert_kernel(tile_e_ref, n_tiles_ref, pos_ref, w_ref, h2t_ref, wg_ref, wu_ref, wd_ref,
                   y_ref, row_off_ref, row_w_ref, buf_ref, *, n_tok, tm):
    t = pl.program_id(0)
    n_sub = buf_ref.shape[0] // tm
    unroll = 8

    @pl.when(t == 0)
    def _():
        def init(i, carry):
            row_off_ref[i] = n_tok * SUBLANES
            row_w_ref[i] = 0.0
            return carry

        lax.fori_loop(0, row_off_ref.shape[0], init, 0, unroll=unroll)

        def scatter(n, carry):
            for slot in range(2):
                p = pos_ref[2 * n + slot]
                row_off_ref[p] = n * SUBLANES
                row_w_ref[p] = w_ref[2 * n + slot]
            return carry

        lax.fori_loop(0, n_tok, scatter, 0, unroll=unroll // 2)

    @pl.when(t < n_tiles_ref[0])
    def _():
        def gather(i, carry):
            off = jnp.minimum(row_off_ref[t * tm + i], (n_tok - 1) * SUBLANES)
            src = pl.ds(pl.multiple_of(off, SUBLANES), SUBLANES)
            buf_ref[pl.ds(pl.multiple_of(i * SUBLANES, SUBLANES), SUBLANES), :] = h2t_ref[src, :]
            return carry

        lax.fori_loop(0, tm, gather, 0, unroll=unroll)
        h = jnp.concatenate([buf_ref[pl.ds(s, tm, stride=SUBLANES), :] for s in range(n_sub)], axis=1).astype(BF16)
        hg = _dot(h, wg_ref[0])
        hu = _dot(h, wu_ref[0])
        act = (hg * jax.nn.sigmoid(hg)) * hu
        y = _dot(act.astype(BF16), wd_ref[0])
        for s in range(n_sub):
            y_ref[pl.ds(s, tm, stride=SUBLANES), :] = y[:, s * LANES:(s + 1) * LANES]

    @pl.when(t >= n_tiles_ref[0])
    def _():
        y_ref[...] = jnp.zeros_like(y_ref)


def _experts(h2t, pos, w_flat, tile_e, n_tiles, t_max, wg, wu, wd, n_tok):
    tm = MOE_TM
    n_exp, d, hid = wg.shape
    n_sub = d // LANES
    rows = t_max * tm
    grid_spec = pltpu.PrefetchScalarGridSpec(
        num_scalar_prefetch=4,
        grid=(t_max,),
        in_specs=[pl.BlockSpec(h2t.shape, lambda t, te, *_: (0, 0), pipeline_mode=pl.Buffered(1)),
                  pl.BlockSpec((1, d, hid), lambda t, te, *_: (te[t], 0, 0)),
                  pl.BlockSpec((1, d, hid), lambda t, te, *_: (te[t], 0, 0)),
                  pl.BlockSpec((1, hid, d), lambda t, te, *_: (te[t], 0, 0))],
        out_specs=[pl.BlockSpec((tm * n_sub, LANES), lambda t, te, *_: (t, 0)),
                   pl.BlockSpec(memory_space=pltpu.SMEM), pl.BlockSpec(memory_space=pltpu.SMEM)],
        scratch_shapes=[pltpu.VMEM((tm * n_sub, LANES), F32)],
    )
    return pl.pallas_call(
        functools.partial(_expert_kernel, n_tok=n_tok, tm=tm),
        grid_spec=grid_spec,
        out_shape=[jax.ShapeDtypeStruct((rows * n_sub, LANES), F32),
                   jax.ShapeDtypeStruct((rows,), jnp.int32), jax.ShapeDtypeStruct((rows,), F32)],
        compiler_params=_cparams(("arbitrary",)),
    )(tile_e, n_tiles, pos, w_flat, h2t, wg, wu, wd)


def _combine_kernel(row_off_ref, row_w_ref, ys_ref, x1_ref, g2_ref, fng_ref, yc_ref, yl_ref, acc_ref, stage_ref,
                    *, t_max, ctx_tiles):
    step = pl.program_id(0)
    tm = row_off_ref.shape[2]
    tmf = x1_ref.shape[0]
    n_sub = x1_ref.shape[1] // LANES
    group = SUBLANES

    @pl.when(step == 0)
    def _():
        acc_ref[...] = jnp.zeros_like(acc_ref)

    @pl.when(step < t_max)
    def _():
        def body(ib, carry):
            slots, vals = [], []
            for k in range(group):
                i = ib * group + k
                slot = pl.ds(pl.multiple_of(row_off_ref[0, 0, i], n_sub), n_sub)
                slots.append(slot)
                vals.append(acc_ref[slot, :]
                            + row_w_ref[0, 0, i] * ys_ref[pl.ds(pl.multiple_of(i * n_sub, n_sub), n_sub), :])
            for slot, val in zip(slots, vals):
                acc_ref[slot, :] = val
            return carry

        lax.fori_loop(0, tm // group, body, 0)

    @pl.when(step >= t_max)
    def _():
        j = step - t_max
        stage_ref[...] = acc_ref[pl.ds(pl.multiple_of(j * tmf * n_sub, tmf * n_sub), tmf * n_sub), :]
        moe = jnp.concatenate([stage_ref[pl.ds(s, tmf, stride=n_sub), :] for s in range(n_sub)], axis=1)
        x2 = x1_ref[...] + g2_ref[0] * moe
        y = x2 * lax.rsqrt(jnp.mean(x2 * x2, axis=-1, keepdims=True) + EPS) * fng_ref[...]

        @pl.when(j < ctx_tiles)
        def _():
            yc_ref[...] = y

        @pl.when(j >= ctx_tiles)
        def _():
            yl_ref[...] = y


def _combine(ys, row_off, row_w, x1, mod3, fng, n_ctx, t_lat):
    n, d = x1.shape
    tm = MOE_TM
    tmf = MOE_TMF
    t_max = row_off.shape[0] // tm
    n_sub = d // LANES
    assert n_sub == SUBLANES, "one token must fill exactly one (8, 128) tile"
    ctx_tiles = n_ctx // tmf
    row = functools.partial(_mod_row, tm=tmf, n_ctx=n_ctx, t_lat=t_lat)
    src = lambda s: jnp.minimum(s, t_max - 1)
    dst = lambda s: jnp.maximum(s - t_max, 0)
    smem_tile = pl.BlockSpec((1, 1, tm), lambda s: (src(s), 0, 0), memory_space=pltpu.SMEM)
    return pl.pallas_call(
        functools.partial(_combine_kernel, t_max=t_max, ctx_tiles=ctx_tiles),
        grid=(t_max + n // tmf,),
        in_specs=[smem_tile, smem_tile,
                  pl.BlockSpec((tm * n_sub, LANES), lambda s: (src(s), 0)),
                  pl.BlockSpec((tmf, d), lambda s: (dst(s), 0)),
                  pl.BlockSpec((1, 1, d), lambda s: (row(dst(s)), 0, 5)),
                  pl.BlockSpec((1, d), lambda s: (0, 0))],
        out_specs=[pl.BlockSpec((tmf, d), lambda s: (jnp.minimum(dst(s), ctx_tiles - 1), 0)),
                   pl.BlockSpec((tmf, d), lambda s: (jnp.maximum(dst(s) - ctx_tiles, 0), 0))],
        out_shape=[jax.ShapeDtypeStruct((n_ctx, d), F32), jax.ShapeDtypeStruct((n - n_ctx, d), F32)],
        scratch_shapes=[pltpu.VMEM(((n + 1) * n_sub, LANES), F32),
                        pltpu.VMEM((tmf * n_sub, LANES), F32)],
        compiler_params=_cparams(("arbitrary",)),
    )(row_off.reshape(t_max, 1, tm), row_w.reshape(t_max, 1, tm), ys, x1, mod3, fng)


def _block_diag2(a, b):
    za = jnp.zeros((a.shape[0], b.shape[1]), a.dtype)
    zb = jnp.zeros((b.shape[0], a.shape[1]), a.dtype)
    return jnp.concatenate([jnp.concatenate([a, za], 1), jnp.concatenate([zb, b], 1)], 0)


def kernel(x_prompt, x_sample, state_rwkv, state_gla, c, c_ctx, w_mod, b_mod, norm1_g, norm2_g, w_in, shift_mu, a_w0, a_w_up, a_a0, a_a_up, a_g_up, a_k_k, a_k_a, a_r_k, a_ln_g, a_ln_b, b_gk_up, b_gk_bias, b_norm_g, w_out_a, w_out_b, w_o, w_router_group, b_router_group, w_router_expert, b_router_expert, w_exp_gate, w_exp_up, w_exp_down, final_norm_g):
    n_ctx_seq, t_ctx, d = x_prompt.shape
    n_lat_seq, t_lat, _ = x_sample.shape
    assert w_mod.shape[0] == 1, "single-layer configuration"
    n_ctx = n_ctx_seq * t_ctx
    n_lat = n_lat_seq * t_lat
    x = jnp.concatenate([x_prompt.reshape(n_ctx, d), x_sample.reshape(n_lat, d)], 0)

    cvec = jnp.concatenate([c_ctx[None], c, jnp.zeros((SUBLANES - 1 - n_lat_seq, d), F32)], 0)
    mod = _modulation(cvec, w_mod[0], b_mod[0][None])
    mod3 = mod[:, None, :]

    a_cols = shift_mu.shape[1]
    b_cols = 2 * B_HEADS * B_KEY + 2 * B_HEADS * B_VAL + 2 * b_gk_up.shape[2]
    b_pad = -b_cols % LANES
    w_in0 = w_in[0].astype(BF16)
    wa = w_in0[:, :a_cols]
    wb = jnp.pad(w_in0[:, a_cols:a_cols + b_cols], ((0, 0), (0, b_pad)))
    wg = w_in0[:, a_cols + b_cols:]
    pa, pb, pg = _in_proj(x, mod3, norm1_g, wa, wb, wg, n_ctx, t_lat)

    seg = jnp.kron(jnp.eye(A_HEADS, dtype=BF16), jnp.ones((A_DIM, A_DIM), BF16))
    consts_a = (shift_mu, _block_diag2(a_w_up[0, 0], a_w_up[0, 1]), _block_diag2(a_a_up[0, 0], a_a_up[0, 1]),
                a_g_up[0], a_w0[0], a_a0[0], a_k_k, a_k_a, a_r_k[0].reshape(1, A_WIDTH), seg)
    prep_c = _prep_a(pa, 0, n_ctx_seq, t_ctx, False, consts_a)
    ra, wk, bv, bonus, g_a = _prep_a(pa, n_ctx, n_lat_seq, t_lat, True, consts_a, prev_outs=prep_c)

    n_tok = n_ctx + n_lat
    assert (n_ctx // t_lat) % n_lat_seq == 0
    view3 = lambda a, t: a.reshape(n_tok // t, t * A_HEADS, LANES)
    view4 = lambda a, t: a.reshape(2, n_tok // t, t * A_HEADS, LANES)
    o_ctx, sfin = _scan_ctx(view3(ra, t_ctx), view4(wk, t_ctx), view4(bv, t_ctx), n_ctx_seq, t_ctx, 16)
    o_ctx = o_ctx.reshape(2, t_ctx, A_DIM, n_ctx_seq, A_HEADS).transpose(0, 3, 1, 4, 2).reshape(2, n_ctx, A_WIDTH)
    new_state_rwkv = sfin.reshape(2, A_DIM, A_DIM, n_ctx_seq, A_HEADS).transpose(3, 0, 4, 2, 1)[:, None]

    vl = A_DIM // 2
    s0 = state_rwkv[:, 0].reshape(n_lat_seq, 2, A_HEADS, 2, vl, A_DIM)
    s0 = s0.transpose(5, 4, 3, 1, 0, 2).reshape(A_DIM, vl, LANES)
    of_l, ob_l = _scan_lat(view3(ra, t_lat), view4(wk, t_lat), view4(bv, t_lat), s0, n_lat_seq, t_lat, 16,
                           n_ctx // t_lat // n_lat_seq)

    def lat_rows(o, dirn):
        o = o.reshape(t_lat, vl, 2, 2, n_lat_seq, A_HEADS)[:, :, :, dirn]
        return o.transpose(3, 0, 4, 2, 1).reshape(n_lat, A_WIDTH)

    o_f = jnp.concatenate([o_ctx[0], lat_rows(of_l, 0)], 0)
    o_b = jnp.concatenate([o_ctx[1], lat_rows(ob_l, 1)], 0)

    rank = b_gk_up.shape[2]
    wgk = jnp.zeros((2, LANES, B_HEADS * B_KEY), F32)
    wgk = wgk.at[0, :rank].set(b_gk_up[0, 0]).at[1, rank:2 * rank].set(b_gk_up[0, 1])
    zero_gla = jnp.zeros((n_ctx_seq, 1, 2, B_HEADS, B_KEY, B_VAL), F32)
    yb_c, new_state_gla = _gla(pb, 0, n_ctx_seq, t_ctx, wgk, b_gk_bias[0], b_norm_g, zero_gla, False)
    yb, _ = _gla(pb, n_ctx, n_lat_seq, t_lat, wgk, b_gk_bias[0], b_norm_g, state_gla, True, yb_prev=yb_c)

    w_r = jnp.concatenate([w_router_group[0], w_router_expert[0].transpose(1, 0, 2).reshape(d, N_EXPERTS)], 1)
    w_r = jnp.pad(w_r, ((0, 0), (0, LANES - w_r.shape[1])))
    b_r = jnp.concatenate([b_router_group[0], b_router_expert[0].reshape(N_EXPERTS)])
    b_r = jnp.pad(b_r, (0, LANES - b_r.shape[0]))[None]
    consts_m = (a_ln_g, a_ln_b, seg, w_out_a[0].astype(BF16), w_out_b[0].astype(BF16), w_o[0].astype(BF16),
                norm2_g, w_r, b_r)
    x1, h2t, route = _mix(o_f, o_b, bonus, g_a, yb, pg, x, mod3, consts_m, n_ctx, t_lat)

    pos, w_flat, tile_e, n_tiles, t_max = _route_tables(route, n_tok)
    ys, row_off, row_w = _experts(h2t, pos, w_flat, tile_e, n_tiles, t_max, w_exp_gate[0].astype(BF16),
                                  w_exp_up[0].astype(BF16), w_exp_down[0].astype(BF16), n_tok)
    y_c, y_l = _combine(ys, row_off, row_w, x1, mod3, final_norm_g[None], n_ctx, t_lat)
    return (y_c.reshape(n_ctx_seq, t_ctx, d), y_l.reshape(n_lat_seq, t_lat, d), new_state_rwkv, new_state_gla)
```
